```python
import math
import jax, jax.numpy as jnp
from jax import lax
import numpy as np

D_MODEL = 1024
BATCH = 8
SEQ = 2048
DEPTH = 4
DEC_BATCH = 128
DEC_SEQ = 1
PAST_LEN = 2048
PAGE_SIZE = 128

N_A = DEPTH // 2
N_B = DEPTH - N_A
D_LRU = D_MODEL
LRU_BLOCK = 128
N_LRU_BLOCKS = D_LRU // LRU_BLOCK
LRU_CONV = 4
LRU_C = 8.0
N_HEADS = 8
HEAD_DIM = D_MODEL // (2 * N_HEADS)
D_FF = 3 * D_MODEL
FFN_CONV = 3
Q_BLOCK = 128
EPS = 1e-6

kernel_name = "yoco_rglru_diffattn_step"


def rms_norm(x, g):
    xf = x.astype(jnp.float32)
    y = xf * lax.rsqrt(jnp.mean(xf * xf, axis=-1, keepdims=True) + EPS)
    return (y * g.astype(jnp.float32)).astype(x.dtype)


def modulated_norm(x, g, shift, scale):
    return rms_norm(x, g) * (1 + scale) + shift


def causal_conv(x, buf, w, b):
    width = w.shape[0]
    t = x.shape[1]
    xp = jnp.concatenate([buf.astype(x.dtype), x], axis=1)
    y = b
    for j in range(width):
        y = y + xp[:, j:j + t] * w[j]
    return y, xp[:, t:]


def linear_scan(a, b, h0):
    def combine(left, right):
        a1, b1 = left
        a2, b2 = right
        return a1 * a2, a2 * b1 + b2
    a_cum, b_cum = lax.associative_scan(combine, (a, b), axis=1)
    return a_cum * h0[:, None] + b_cum


def rglru_block(h, p, i, h0, conv0):
    bsz, t, _ = h.shape
    f32 = jnp.float32
    xb, gb = jnp.split(h @ p['w_lru_in'][i], 2, axis=-1)
    xc, conv_new = causal_conv(xb, conv0, p['w_lru_conv'][i], p['b_lru_conv'][i])
    xblk = xc.reshape(bsz, t, N_LRU_BLOCKS, LRU_BLOCK)
    gate_x = jax.nn.sigmoid(jnp.einsum('btni,nij->btnj', xblk, p['w_gate_x'][i]).reshape(bsz, t, D_LRU) + p['b_gate_x'][i])
    gate_a = jax.nn.sigmoid(jnp.einsum('btni,nij->btnj', xblk, p['w_gate_a'][i]).reshape(bsz, t, D_LRU) + p['b_gate_a'][i])
    log_a = LRU_C * gate_a.astype(f32) * jax.nn.log_sigmoid(p['lru_log_param'][i].astype(f32))
    a = jnp.exp(log_a)
    b = jnp.sqrt(-jnp.expm1(2.0 * log_a)) * (gate_x * xc).astype(f32)
    hs = linear_scan(a, b, h0.astype(f32))
    y = hs.astype(h.dtype) * jax.nn.gelu(gb)
    return y @ p['w_lru_out'][i], hs[:, -1].astype(h.dtype), conv_new


def conv_ffn(h, p, l, buf):
    u = h @ p['w_up'][l]
    u, buf_new = causal_conv(u, buf, p['w_ffn_conv'][l], p['b_ffn_conv'][l])
    a, g = jnp.split(u, 2, axis=-1)
    return (jax.nn.gelu(a) * g) @ p['w_down'][l], buf_new


def alibi_slopes():
    return 2.0 ** (-8.0 * jnp.arange(1, N_HEADS + 1, dtype=jnp.float32) / N_HEADS)


def diff_attention(q, q_pos, segments, lam, slopes):
    scale = HEAD_DIM ** -0.5
    scores = []
    for k, _, k_pos in segments:
        s = jnp.einsum('bqhcd,bkhcd->bhcqk', q, k, preferred_element_type=jnp.float32) * scale
        dist = (q_pos[:, None] - k_pos[None, :]).astype(jnp.float32)
        s = s - slopes[:, None, None, None] * dist
        scores.append(jnp.where(dist >= 0, s, -jnp.inf))
    probs = jax.nn.softmax(jnp.concatenate(scores, axis=-1), axis=-1)
    attn = probs[:, :, 0] - lam * probs[:, :, 1]
    out = None
    off = 0
    for _, v, k_pos in segments:
        n = k_pos.shape[0]
        part = jnp.einsum('bhqk,bkhe->bqhe', attn[..., off:off + n].astype(v.dtype), v,
                          preferred_element_type=jnp.float32)
        out = part if out is None else out + part
        off += n
    return out.astype(q.dtype)


def prompt_attention(q, k, v, lam, slopes):
    bsz, s = q.shape[:2]
    nb = s // Q_BLOCK
    qb = jnp.moveaxis(q.reshape(bsz, nb, Q_BLOCK, N_HEADS, 2, HEAD_DIM), 1, 0)
    k_pos = jnp.arange(s)

    def one_block(args):
        qi, bi = args
        q_pos = bi * Q_BLOCK + jnp.arange(Q_BLOCK)
        return diff_attention(qi, q_pos, [(k, v, k_pos)], lam, slopes)

    out = lax.map(one_block, (qb, jnp.arange(nb)))
    return jnp.moveaxis(out, 0, 1).reshape(bsz, s, N_HEADS, 2 * HEAD_DIM)


def shared_kv(x, sc, p):
    bsz, t, _ = x.shape
    shift, scale = jnp.split((sc @ p['w_ada_kv'] + p['b_ada_kv'])[:, None], 2, axis=-1)
    h = modulated_norm(x, p['g_norm_kv'], shift, scale)
    k, v = jnp.split(h @ p['w_kv'], 2, axis=-1)
    k = rms_norm(k.reshape(bsz, t, N_HEADS, 2, HEAD_DIM), p['g_k_norm'])
    v = v.reshape(bsz, t, N_HEADS, 2 * HEAD_DIM)
    return k, v


def diff_attn_layer(h, p, j, lam_init, attend):
    bsz, t, _ = h.shape
    f32 = jnp.float32
    q = rms_norm((h @ p['w_q'][j]).reshape(bsz, t, N_HEADS, 2, HEAD_DIM), p['g_q_norm'][j])
    lam = (jnp.exp(jnp.sum(p['lam_q1'][j].astype(f32) * p['lam_k1'][j].astype(f32)))
           - jnp.exp(jnp.sum(p['lam_q2'][j].astype(f32) * p['lam_k2'][j].astype(f32))) + lam_init)
    o = attend(q, lam)
    o = rms_norm(o, p['g_subln'][j]) * (1.0 - lam_init)
    return o.reshape(bsz, t, N_HEADS * 2 * HEAD_DIM) @ p['w_o'][j]


def run_group(x, c, pos0, lru_h0, lru_conv0, ffn_conv0, kv_past, p):
    t = x.shape[1]
    slopes = alibi_slopes()
    q_pos = pos0 + jnp.arange(t)
    sc = jax.nn.silu(c)
    lru_h, lru_conv, ffn_conv = [], [], []
    k_new = v_new = None
    attend = None
    for l in range(DEPTH):
        mod = (sc @ p['w_ada'][l] + p['b_ada'][l])[:, None]
        sh1, s1, g1, sh2, s2, g2 = jnp.split(mod, 6, axis=-1)
        h = modulated_norm(x, p['g_norm_mix'][l], sh1, s1)
        if l < N_A:
            out, h_last, conv_new = rglru_block(h, p, l, lru_h0[l], lru_conv0[l])
            lru_h.append(h_last)
            lru_conv.append(conv_new)
        else:
            lam_init = 0.8 - 0.6 * math.exp(-0.3 * l)
            out = diff_attn_layer(h, p, l - N_A, lam_init, attend)
        x = x + g1 * out
        h = modulated_norm(x, p['g_norm_ffn'][l], sh2, s2)
        out, fb = conv_ffn(h, p, l, ffn_conv0[l])
        ffn_conv.append(fb)
        x = x + g2 * out
        if l == N_A - 1:
            k_new, v_new = shared_kv(x, sc, p)
            if kv_past is None:
                def attend(q, lam, k=k_new, v=v_new):
                    return prompt_attention(q, k, v, lam, slopes)
            else:
                k_past, v_past = kv_past
                past_pos = jnp.arange(k_past.shape[1])

                def attend(q, lam, k=k_new, v=v_new):
                    return diff_attention(q, q_pos, [(k_past, v_past, past_pos), (k, v, q_pos)], lam, slopes)
    return x, k_new, v_new, jnp.stack(lru_h), jnp.stack(lru_conv), jnp.stack(ffn_conv)


def setup_inputs(seed: int = 0) -> dict:
    key = jax.random.key(seed)
    ks = iter(jax.random.split(key, 64))

    def nrm(shape, scale):
        return jax.random.normal(next(ks), shape, jnp.float32) * scale

    def gain(shape):
        return 1.0 + nrm(shape, 0.02)

    n_pages = PAST_LEN // PAGE_SIZE
    n_used = DEC_BATCH * n_pages
    n_pool = n_used + n_used // 4
    page_table = jax.random.permutation(next(ks), n_pool)[:n_used].reshape(DEC_BATCH, n_pages).astype(jnp.int32)
    a0 = jax.random.uniform(next(ks), (N_A, D_LRU), jnp.float32, 0.9, 0.999)
    s0 = a0 ** (1.0 / LRU_C)
    lru_log_param = jnp.log(s0) - jnp.log1p(-s0)
    d = D_MODEL
    attn_w = 2 * N_HEADS * HEAD_DIM
    return {
        'x_prompt': nrm((BATCH, SEQ, d), 1.0),
        'x_sample': nrm((DEC_BATCH, DEC_SEQ, d), 1.0),
        'c_prompt': nrm((BATCH, d), 1.0),
        'c_sample': nrm((DEC_BATCH, d), 1.0),
        'cache_k': nrm((n_pool, PAGE_SIZE, N_HEADS, 2, HEAD_DIM), 1.0),
        'cache_v': nrm((n_pool, PAGE_SIZE, N_HEADS, 2 * HEAD_DIM), 1.0),
        'page_table': page_table,
        'state_lru_h': nrm((N_A, DEC_BATCH, D_LRU), 0.5),
        'state_lru_conv': nrm((N_A, DEC_BATCH, LRU_CONV - 1, D_LRU), 1.0),
        'state_ffn_conv': nrm((DEPTH, DEC_BATCH, FFN_CONV - 1, 2 * D_FF), 1.0),
        'w_ada': nrm((DEPTH, d, 6 * d), 0.5 * d ** -0.5),
        'b_ada': nrm((DEPTH, 6 * d), 0.02),
        'g_norm_mix': gain((DEPTH, d)),
        'g_norm_ffn': gain((DEPTH, d)),
        'w_lru_in': nrm((N_A, d, 2 * D_LRU), d ** -0.5),
        'w_lru_conv': nrm((N_A, LRU_CONV, D_LRU), LRU_CONV ** -0.5),
        'b_lru_conv': nrm((N_A, D_LRU), 0.02),
        'w_gate_x': nrm((N_A, N_LRU_BLOCKS, LRU_BLOCK, LRU_BLOCK), LRU_BLOCK ** -0.5),
        'b_gate_x': nrm((N_A, D_LRU), 0.02),
        'w_gate_a': nrm((N_A, N_LRU_BLOCKS, LRU_BLOCK, LRU_BLOCK), LRU_BLOCK ** -0.5),
        'b_gate_a': nrm((N_A, D_LRU), 0.02),
        'lru_log_param': lru_log_param,
        'w_lru_out': nrm((N_A, D_LRU, d), D_LRU ** -0.5),
        'w_ada_kv': nrm((d, 2 * d), 0.5 * d ** -0.5),
        'b_ada_kv': nrm((2 * d,), 0.02),
        'g_norm_kv': gain((d,)),
        'w_kv': nrm((d, 2 * attn_w), d ** -0.5),
        'g_k_norm': gain((HEAD_DIM,)),
        'w_q': nrm((N_B, d, attn_w), d ** -0.5),
        'g_q_norm': gain((N_B, HEAD_DIM)),
        'lam_q1': nrm((N_B, HEAD_DIM), 0.1),
        'lam_k1': nrm((N_B, HEAD_DIM), 0.1),
        'lam_q2': nrm((N_B, HEAD_DIM), 0.1),
        'lam_k2': nrm((N_B, HEAD_DIM), 0.1),
        'g_subln': gain((N_B, 2 * HEAD_DIM)),
        'w_o': nrm((N_B, attn_w, d), attn_w ** -0.5),
        'w_up': nrm((DEPTH, d, 2 * D_FF), d ** -0.5),
        'w_ffn_conv': nrm((DEPTH, FFN_CONV, 2 * D_FF), FFN_CONV ** -0.5),
        'b_ffn_conv': nrm((DEPTH, 2 * D_FF), 0.02),
        'w_down': nrm((DEPTH, D_FF, d), D_FF ** -0.5),
    }


def reference(x_prompt, x_sample, c_prompt, c_sample, cache_k, cache_v, page_table,
              state_lru_h, state_lru_conv, state_ffn_conv,
              w_ada, b_ada, g_norm_mix, g_norm_ffn,
              w_lru_in, w_lru_conv, b_lru_conv, w_gate_x, b_gate_x, w_gate_a, b_gate_a,
              lru_log_param, w_lru_out,
              w_ada_kv, b_ada_kv, g_norm_kv, w_kv, g_k_norm,
              w_q, g_q_norm, lam_q1, lam_k1, lam_q2, lam_k2, g_subln, w_o,
              w_up, w_ffn_conv, b_ffn_conv, w_down):
    p = dict(w_ada=w_ada, b_ada=b_ada, g_norm_mix=g_norm_mix, g_norm_ffn=g_norm_ffn,
             w_lru_in=w_lru_in, w_lru_conv=w_lru_conv, b_lru_conv=b_lru_conv,
             w_gate_x=w_gate_x, b_gate_x=b_gate_x, w_gate_a=w_gate_a, b_gate_a=b_gate_a,
             lru_log_param=lru_log_param, w_lru_out=w_lru_out,
             w_ada_kv=w_ada_kv, b_ada_kv=b_ada_kv, g_norm_kv=g_norm_kv, w_kv=w_kv, g_k_norm=g_k_norm,
             w_q=w_q, g_q_norm=g_q_norm, lam_q1=lam_q1, lam_k1=lam_k1, lam_q2=lam_q2, lam_k2=lam_k2,
             g_subln=g_subln, w_o=w_o, w_up=w_up, w_ffn_conv=w_ffn_conv, b_ffn_conv=b_ffn_conv,
             w_down=w_down)
    bp = x_prompt.shape[0]
    dt = x_prompt.dtype
    h0_p = jnp.zeros((N_A, bp, D_LRU), dt)
    lc0_p = jnp.zeros((N_A, bp, LRU_CONV - 1, D_LRU), dt)
    fc0_p = jnp.zeros((DEPTH, bp, FFN_CONV - 1, 2 * D_FF), dt)
    y_prompt, k_prompt, v_prompt, h_p, lc_p, fc_p = run_group(
        x_prompt, c_prompt, 0, h0_p, lc0_p, fc0_p, None, p)
    bs = x_sample.shape[0]
    past_len = page_table.shape[1] * PAGE_SIZE
    k_past = cache_k[page_table].reshape(bs, past_len, N_HEADS, 2, HEAD_DIM)
    v_past = cache_v[page_table].reshape(bs, past_len, N_HEADS, 2 * HEAD_DIM)
    y_sample, k_sample, v_sample, h_s, lc_s, fc_s = run_group(
        x_sample, c_sample, past_len, state_lru_h, state_lru_conv, state_ffn_conv, (k_past, v_past), p)
    return (y_prompt, y_sample, k_prompt, v_prompt, k_sample, v_sample,
            h_p, h_s, lc_p, lc_s, fc_p, fc_s)
```

```python
import functools
import math

import numpy as np
import jax
import jax.numpy as jnp
from jax import lax
from jax.experimental import pallas as pl
from jax.experimental.pallas import tpu as pltpu

f32 = jnp.float32
bf16 = jnp.bfloat16

EPS = 1e-6
LRU_C = 8.0
LANES = 128
PAGE_SIZE = 128
MASKED = -1e30
VMEM_LIMIT_BYTES = 56 * 1024 * 1024


def _params(*semantics):
    return pltpu.CompilerParams(dimension_semantics=semantics,
                                vmem_limit_bytes=VMEM_LIMIT_BYTES)


def _const_spec(shape):
    zeros = (0,) * len(shape)
    return pl.BlockSpec(shape, lambda *_: zeros)


def _sigmoid(x):
    return 1.0 / (1.0 + jnp.exp(-x))


def _gelu_tanh(x):
    return 0.5 * x * (1.0 + jnp.tanh(math.sqrt(2.0 / math.pi) * (x + 0.044715 * (x * x * x))))


def _mod_norm(x, gain, shift, scale, group):
    rows, d = x.shape
    y = x * lax.rsqrt(jnp.mean(x * x, axis=-1, keepdims=True) + EPS) * gain
    y = y.reshape(rows // group, group, d) * (1.0 + scale)[None] + shift[None]
    return y.reshape(rows, d)


def _gate(x, gate, group):
    rows, d = x.shape
    return (x.reshape(rows // group, group, d) * gate[None]).reshape(rows, d)


def _head_half_norm(y, gain2):
    lane = lax.broadcasted_iota(jnp.int32, y.shape, 1)
    low = lane < (LANES // 2)
    sq = y * y
    s_low = jnp.sum(jnp.where(low, sq, 0.0), axis=-1, keepdims=True)
    s_high = jnp.sum(jnp.where(low, 0.0, sq), axis=-1, keepdims=True)
    inv = lax.rsqrt(jnp.where(low, s_low, s_high) * (2.0 / LANES) + EPS)
    return y * inv * gain2


def _ada_kernel(c_ref, w_ref, b_ref, o_ref):
    c = c_ref[...]
    sc = (c * _sigmoid(c)).astype(bf16)
    o_ref[0] = jnp.dot(sc, w_ref[0].astype(bf16), preferred_element_type=f32) + b_ref[0]


def _ada(c, w, b, tn):
    n_layers, d, n = w.shape
    m = c.shape[0]
    return pl.pallas_call(
        _ada_kernel,
        out_shape=jax.ShapeDtypeStruct((n_layers, m, n), f32),
        grid=(n_layers, n // tn),
        in_specs=[_const_spec((m, d)),
                  pl.BlockSpec((1, d, tn), lambda l, j: (l, 0, j)),
                  pl.BlockSpec((1, 1, tn), lambda l, j: (l, 0, j))],
        out_specs=pl.BlockSpec((1, m, tn), lambda l, j: (l, 0, j)),
        compiler_params=_params("arbitrary", "arbitrary"),
        name="ada_mod",
    )(c, w, b)


def _mod_spec(group, row_block, layer, piece):
    def index(*_):
        return (layer, row_block, piece)
    return lambda d: pl.BlockSpec((1, group, d), index)


def _lru_kernel(group, steps, n_conv,
                x_ref, sh_ref, sc_ref, gt_ref, gn_ref, win_ref, cw_ref, cb_ref,
                wgx_ref, bgx_ref, wga_ref, bga_ref, lp_ref, wout_ref, h0_ref, c0_ref,
                o_ref, hlast_ref, cnew_ref,
                xw_scr, halo_scr, a_scr, b_scr, gb_scr, y_scr, h_scr):
    rows = group * steps
    d_lru = a_scr.shape[1]
    halo = (n_conv - 1) * group
    t = pl.program_id(0)

    @pl.when(t == 0)
    def _():
        h_scr[...] = h0_ref[...]
        halo_scr[...] = c0_ref[...]

    x = x_ref[...]
    h = _mod_norm(x, gn_ref[...], sh_ref[0], sc_ref[0], group).astype(bf16)
    u = jnp.dot(h, win_ref[...], preferred_element_type=f32)
    gb_scr[...] = u[:, d_lru:]

    xw_scr[0:halo, :] = halo_scr[...]
    xw_scr[halo:halo + rows, :] = u[:, :d_lru]
    xc = cb_ref[...] + cw_ref[0:1, :] * xw_scr[0:rows, :]
    for j in range(1, n_conv):
        xc = xc + cw_ref[j:j + 1, :] * xw_scr[j * group:j * group + rows, :]
    new_halo = xw_scr[rows:rows + halo, :]
    halo_scr[...] = new_halo
    cnew_ref[...] = new_halo

    lp = lp_ref[...]
    log_sig = jnp.minimum(lp, 0.0) - jnp.log1p(jnp.exp(-jnp.abs(lp)))
    n_blocks = wgx_ref.shape[0]
    blk = d_lru // n_blocks
    for n in range(n_blocks):
        cols = slice(n * blk, (n + 1) * blk)
        xn = xc[:, cols]
        xn16 = xn.astype(bf16)
        gate_x = _sigmoid(jnp.dot(xn16, wgx_ref[n], preferred_element_type=f32) + bgx_ref[:, cols])
        gate_a = _sigmoid(jnp.dot(xn16, wga_ref[n], preferred_element_type=f32) + bga_ref[:, cols])
        log_a = LRU_C * gate_a * log_sig[:, cols]
        a = jnp.exp(log_a)
        a_scr[:, cols] = a
        b_scr[:, cols] = jnp.sqrt(-jnp.tanh(log_a) * (a * a + 1.0)) * (gate_x * xn)

    def step(i, hprev):
        r = pl.multiple_of(i * group, group)
        hnew = a_scr[pl.ds(r, group), :] * hprev + b_scr[pl.ds(r, group), :]
        b_scr[pl.ds(r, group), :] = hnew
        return hnew

    if steps == 1:
        hlast = step(0, h_scr[...])
    else:
        hlast = lax.fori_loop(0, steps, step, h_scr[...], unroll=8)
    h_scr[...] = hlast
    hlast_ref[...] = hlast

    y_scr[...] = (b_scr[...] * _gelu_tanh(gb_scr[...])).astype(bf16)
    out = jnp.dot(y_scr[...], wout_ref[...], preferred_element_type=f32)
    o_ref[...] = x + _gate(out, gt_ref[0], group)


def _lru_layer(x, mod, mod_rows, layer, group, steps, gn, win, cw, cb, wgx, bgx, wga, bga, lp, wout, h0, c0):
    rows_total, d = x.shape
    d_lru = wout.shape[0]
    n_conv = cw.shape[0]
    rows = group * steps
    halo = (n_conv - 1) * group
    spec = lambda p: _mod_spec(group, mod_rows, layer, p)(d)
    tile = pl.BlockSpec((rows, d), lambda t: (t, 0))
    kernel = functools.partial(_lru_kernel, group, steps, n_conv)
    return pl.pallas_call(
        kernel,
        out_shape=(jax.ShapeDtypeStruct((rows_total, d), f32),
                   jax.ShapeDtypeStruct((group, d_lru), f32),
                   jax.ShapeDtypeStruct((halo, d_lru), f32)),
        grid=(rows_total // rows,),
        in_specs=[tile, spec(0), spec(1), spec(2), _const_spec(gn.shape), _const_spec(win.shape),
                  _const_spec(cw.shape), _const_spec(cb.shape), _const_spec(wgx.shape), _const_spec(bgx.shape),
                  _const_spec(wga.shape), _const_spec(bga.shape), _const_spec(lp.shape), _const_spec(wout.shape),
                  _const_spec(h0.shape), _const_spec(c0.shape)],
        out_specs=(tile, _const_spec((group, d_lru)), _const_spec((halo, d_lru))),
        scratch_shapes=[pltpu.VMEM((rows + halo, d_lru), f32), pltpu.VMEM((halo, d_lru), f32),
                        pltpu.VMEM((rows, d_lru), f32), pltpu.VMEM((rows, d_lru), f32),
                        pltpu.VMEM((rows, d_lru), f32), pltpu.VMEM((rows, d_lru), bf16),
                        pltpu.VMEM((group, d_lru), f32)],
        compiler_params=_params("arbitrary"),
        name="rglru_layer",
    )(x, mod, mod, mod, gn, win, cw, cb, wgx, bgx, wga, bga, lp, wout, h0, c0)


def _ffn_kernel(group, steps, n_conv,
                x_ref, sh_ref, sc_ref, gt_ref, gn_ref, wa_ref, wg_ref, cwa_ref, cwg_ref, cba_ref, cbg_ref,
                wd_ref, c0a_ref, c0g_ref,
                o_ref, cna_ref, cng_ref,
                h_scr, acc_scr, uwa_scr, uwg_scr, haloa_scr, halog_scr):
    rows = group * steps
    halo = (n_conv - 1) * group
    t = pl.program_id(0)
    c = pl.program_id(1)

    @pl.when(c == 0)
    def _():
        h_scr[...] = _mod_norm(x_ref[...], gn_ref[...], sh_ref[0], sc_ref[0], group).astype(bf16)
        acc_scr[...] = jnp.zeros_like(acc_scr)

    @pl.when(t == 0)
    def _():
        haloa_scr[c] = c0a_ref[...]
        halog_scr[c] = c0g_ref[...]

    h = h_scr[...]

    def conv_half(w_ref, cw_ref, cb_ref, uw_scr, halo_scr, cn_ref):
        uw_scr[0:halo, :] = halo_scr[c]
        uw_scr[halo:halo + rows, :] = jnp.dot(h, w_ref[...], preferred_element_type=f32)
        y = cb_ref[...] + cw_ref[0:1, :] * uw_scr[0:rows, :]
        for j in range(1, n_conv):
            y = y + cw_ref[j:j + 1, :] * uw_scr[j * group:j * group + rows, :]
        new_halo = uw_scr[rows:rows + halo, :]
        halo_scr[c] = new_halo
        cn_ref[...] = new_halo
        return y

    a = conv_half(wa_ref, cwa_ref, cba_ref, uwa_scr, haloa_scr, cna_ref)
    g = conv_half(wg_ref, cwg_ref, cbg_ref, uwg_scr, halog_scr, cng_ref)
    act = (_gelu_tanh(a) * g).astype(bf16)
    acc_scr[...] += jnp.dot(act, wd_ref[...], preferred_element_type=f32)

    @pl.when(c == pl.num_programs(1) - 1)
    def _():
        o_ref[...] = x_ref[...] + _gate(acc_scr[...], gt_ref[0], group)


def _ffn_layer(x, mod, mod_rows, layer, group, steps, fc, gn, wup, cw, cb, wdown, c0):
    rows_total, d = x.shape
    d_ff = wdown.shape[0]
    n_conv = cw.shape[0]
    rows = group * steps
    halo = (n_conv - 1) * group
    nc = d_ff // fc
    spec = lambda p: _mod_spec(group, mod_rows, layer, p)(d)
    tile = pl.BlockSpec((rows, d), lambda t, c: (t, 0))
    col_a = lambda r: pl.BlockSpec((r, fc), lambda t, c: (0, c))
    col_g = lambda r: pl.BlockSpec((r, fc), lambda t, c: (0, nc + c))
    kernel = functools.partial(_ffn_kernel, group, steps, n_conv)
    out, cna, cng = pl.pallas_call(
        kernel,
        out_shape=(jax.ShapeDtypeStruct((rows_total, d), f32),
                   jax.ShapeDtypeStruct((halo, d_ff), f32),
                   jax.ShapeDtypeStruct((halo, d_ff), f32)),
        grid=(rows_total // rows, nc),
        in_specs=[tile, spec(3), spec(4), spec(5), _const_spec(gn.shape),
                  col_a(d), col_g(d), col_a(n_conv), col_g(n_conv), col_a(1), col_g(1),
                  pl.BlockSpec((fc, d), lambda t, c: (c, 0)), col_a(halo), col_g(halo)],
        out_specs=(tile, col_a(halo), col_a(halo)),
        scratch_shapes=[pltpu.VMEM((rows, d), bf16), pltpu.VMEM((rows, d), f32),
                        pltpu.VMEM((rows + halo, fc), f32), pltpu.VMEM((rows + halo, fc), f32),
                        pltpu.VMEM((nc, halo, fc), f32), pltpu.VMEM((nc, halo, fc), f32)],
        compiler_params=_params("arbitrary", "arbitrary"),
        name="conv_ffn",
    )(x, mod, mod, mod, gn, wup, wup, cw, cw, cb, cb, wdown, c0, c0)
    return out, jnp.concatenate([cna, cng], axis=-1)


def _q_kernel(group, x_ref, sh_ref, sc_ref, gn_ref, w_ref, gq_ref, o_ref):
    h = _mod_norm(x_ref[...], gn_ref[...], sh_ref[0], sc_ref[0], group).astype(bf16)
    q = jnp.dot(h, w_ref[...], preferred_element_type=f32)
    head_scale = (LANES // 2) ** -0.5
    for j in range(q.shape[1] // LANES):
        cols = slice(j * LANES, (j + 1) * LANES)
        o_ref[:, cols] = (_head_half_norm(q[:, cols], gq_ref[...]) * head_scale).astype(o_ref.dtype)


def _q_proj(x, mod, mod_rows, layer, group, steps, gn, wq, gq2, out_dtype):
    rows_total, d = x.shape
    width = wq.shape[1]
    rows = group * steps
    spec = lambda p: _mod_spec(group, mod_rows, layer, p)(d)
    return pl.pallas_call(
        functools.partial(_q_kernel, group),
        out_shape=jax.ShapeDtypeStruct((rows_total, width), out_dtype),
        grid=(rows_total // rows,),
        in_specs=[pl.BlockSpec((rows, d), lambda t: (t, 0)), spec(0), spec(1), _const_spec(gn.shape),
                  _const_spec(wq.shape), _const_spec(gq2.shape)],
        out_specs=pl.BlockSpec((rows, width), lambda t: (t, 0)),
        compiler_params=_params("arbitrary"),
        name="q_proj",
    )(x, mod, mod, gn, wq, gq2)


def _kv_kernel(group, x_ref, sh_ref, sc_ref, gn_ref, w_ref, gk_ref, k_ref, v_ref, k16_ref, v16_ref):
    h = _mod_norm(x_ref[...], gn_ref[...], sh_ref[0], sc_ref[0], group).astype(bf16)
    kv = jnp.dot(h, w_ref[...], preferred_element_type=f32)
    width = k_ref.shape[1]
    for j in range(width // LANES):
        cols = slice(j * LANES, (j + 1) * LANES)
        kj = _head_half_norm(kv[:, cols], gk_ref[...])
        k_ref[:, cols] = kj
        k16_ref[:, cols] = kj.astype(bf16)
    v = kv[:, width:]
    v_ref[...] = v
    v16_ref[...] = v.astype(bf16)


def _kv_proj(x, mod_kv, mod_rows, group, steps, gn, wkv, gk2):
    rows_total, d = x.shape
    width = wkv.shape[1] // 2
    rows = group * steps
    spec = lambda p: _mod_spec(group, mod_rows, 0, p)(d)
    tile = pl.BlockSpec((rows, d), lambda t: (t, 0))
    out_tile = pl.BlockSpec((rows, width), lambda t: (t, 0))
    return pl.pallas_call(
        functools.partial(_kv_kernel, group),
        out_shape=(jax.ShapeDtypeStruct((rows_total, width), f32), jax.ShapeDtypeStruct((rows_total, width), f32),
                   jax.ShapeDtypeStruct((rows_total, width), bf16), jax.ShapeDtypeStruct((rows_total, width), bf16)),
        grid=(rows_total // rows,),
        in_specs=[tile, spec(0), spec(1), _const_spec(gn.shape), _const_spec(wkv.shape), _const_spec(gk2.shape)],
        out_specs=(out_tile, out_tile, out_tile, out_tile),
        compiler_params=_params("arbitrary"),
        name="kv_proj",
    )(x, mod_kv, mod_kv, gn, wkv, gk2)


def _o_kernel(group, a_ref, x_ref, gt_ref, w_ref, o_ref):
    out = jnp.dot(a_ref[...], w_ref[...], preferred_element_type=f32)
    o_ref[...] = x_ref[...] + _gate(out, gt_ref[0], group)


def _o_proj(attn, x, mod, mod_rows, layer, group, steps, wo):
    rows_total, d = x.shape
    width = wo.shape[0]
    rows = group * steps
    tile = pl.BlockSpec((rows, d), lambda t: (t, 0))
    return pl.pallas_call(
        functools.partial(_o_kernel, group),
        out_shape=jax.ShapeDtypeStruct((rows_total, d), f32),
        grid=(rows_total // rows,),
        in_specs=[pl.BlockSpec((rows, width), lambda t: (t, 0)), tile,
                  _mod_spec(group, mod_rows, layer, 2)(d), _const_spec(wo.shape)],
        out_specs=tile,
        compiler_params=_params("arbitrary"),
        name="o_proj",
    )(attn, x, mod, wo)


def _lambda(lp, lam_init):
    s1 = jnp.sum(lp[0:1] * lp[1:2], axis=-1, keepdims=True)
    s2 = jnp.sum(lp[2:3] * lp[3:4], axis=-1, keepdims=True)
    return jnp.exp(s1) - jnp.exp(s2) + lam_init


def _sub_norm(o, gain, lam_init):
    return o * lax.rsqrt(jnp.mean(o * o, axis=-1, keepdims=True) + EPS) * gain * (1.0 - lam_init)


def _prompt_attn_kernel(lam_init, tq, tk, slopes_ref, lp_ref, gs_ref, q_ref, k_ref, v_ref, o_ref):
    head = pl.program_id(1)
    qi = pl.program_id(2)
    slope = slopes_ref[head]
    lam = _lambda(lp_ref[...], lam_init)

    q = q_ref[...]
    lane = lax.broadcasted_iota(jnp.int32, q.shape, 1)
    zero = jnp.zeros_like(q)
    q_maps = (jnp.where(lane < LANES // 2, q, zero), jnp.where(lane < LANES // 2, zero, q))
    row_minus_col = (lax.broadcasted_iota(jnp.int32, (tq, tk), 0)
                     - lax.broadcasted_iota(jnp.int32, (tq, tk), 1))

    def body(j, carry):
        start = pl.multiple_of(j * tk, tk)
        kt = k_ref[pl.ds(start, tk), :]
        vt = v_ref[pl.ds(start, tk), :]
        dist = row_minus_col + (qi * tq - j * tk)
        bias = -slope * dist.astype(f32)
        visible = dist >= 0
        new = []
        for c in range(2):
            m, l, acc = carry[c]
            s = lax.dot_general(q_maps[c], kt, (((1,), (1,)), ((), ())), preferred_element_type=f32)
            s = jnp.where(visible, s + bias, MASKED)
            m_new = jnp.maximum(m, jnp.max(s, axis=-1, keepdims=True))
            alpha = jnp.exp(m - m_new)
            p = jnp.exp(s - m_new)
            l_new = alpha * l + jnp.sum(p, axis=-1, keepdims=True)
            acc_new = alpha * acc + jnp.dot(p.astype(bf16), vt, preferred_element_type=f32)
            new.append((m_new, l_new, acc_new))
        return tuple(new)

    init = tuple((jnp.full((tq, 1), MASKED, f32), jnp.zeros((tq, 1), f32), jnp.zeros((tq, LANES), f32))
                 for _ in range(2))
    n_kv = ((qi + 1) * tq + tk - 1) // tk
    (m0, l0, acc0), (m1, l1, acc1) = lax.fori_loop(0, n_kv, body, init)
    o = acc0 / l0 - lam * (acc1 / l1)
    o_ref[...] = _sub_norm(o, gs_ref[...], lam_init).astype(o_ref.dtype)


def _prompt_attention(q, k, v, slopes, lam_params, gsub, lam_init, n_seq, n_heads, tq, tk):
    t_len = q.shape[0]
    q_spec = pl.BlockSpec((tq, LANES), lambda g, h, i: (i, g * n_heads + h))
    kv_spec = pl.BlockSpec((t_len, LANES), lambda g, h, i: (0, g * n_heads + h))
    return pl.pallas_call(
        functools.partial(_prompt_attn_kernel, lam_init, tq, tk),
        out_shape=jax.ShapeDtypeStruct(q.shape, bf16),
        grid=(n_seq, n_heads, t_len // tq),
        in_specs=[pl.BlockSpec(memory_space=pltpu.SMEM), _const_spec(lam_params.shape), _const_spec(gsub.shape),
                  q_spec, kv_spec, kv_spec],
        out_specs=q_spec,
        compiler_params=_params("arbitrary", "arbitrary", "arbitrary"),
        name="prompt_attention",
    )(slopes, lam_params, gsub, q, k, v)


def _decode_attn_kernel(lam_init, n_pages, past_len,
                        pt_ref, lp_ref, gs_ref, slope_ref, seg_ref, e0_ref, e1_ref, e0f_ref, e1f_ref,
                        q_ref, kn_ref, vn_ref, *refs):
    k_refs = refs[:n_pages]
    v_refs = refs[n_pages:2 * n_pages]
    o_ref = refs[2 * n_pages]
    m_scr, l_scr, acc0_scr, acc1_scr = refs[2 * n_pages + 1:]
    pg = pl.program_id(1)
    n_maps = seg_ref.shape[1]
    width = q_ref.shape[2]
    sub = acc0_scr.shape[0]

    @pl.when(pg == 0)
    def _():
        m_scr[...] = jnp.full_like(m_scr, MASKED)
        l_scr[...] = jnp.zeros_like(l_scr)
        acc0_scr[...] = jnp.zeros_like(acc0_scr)
        acc1_scr[...] = jnp.zeros_like(acc1_scr)

    q = q_ref[0]
    slope = slope_ref[...]

    def expand(a, ef_ref):
        out = jnp.zeros((1, width), f32)
        for j in range(n_maps):
            out = out + a[:, j:j + 1] * ef_ref[j:j + 1, :]
        return out

    def scores(k_rows):
        return jnp.dot((k_rows * q).astype(bf16), seg_ref[...], preferred_element_type=f32)

    def accumulate(s_list, v_list):
        s_all = jnp.concatenate(s_list, axis=0) if len(s_list) > 1 else s_list[0]
        m_old = m_scr[...]
        m_new = jnp.maximum(m_old, jnp.max(s_all, axis=0, keepdims=True))
        alpha = jnp.exp(m_old - m_new)
        m_scr[...] = m_new
        l_new = alpha * l_scr[...]
        part0 = jnp.zeros((sub, width), f32)
        part1 = jnp.zeros((sub, width), f32)
        for s, vrows in zip(s_list, v_list):
            p = jnp.exp(s - m_new)
            l_new = l_new + jnp.sum(p, axis=0, keepdims=True)
            p16 = p.astype(bf16)
            w0 = jnp.dot(p16, e0_ref[...], preferred_element_type=f32) * vrows
            w1 = jnp.dot(p16, e1_ref[...], preferred_element_type=f32) * vrows
            part0 = part0 + jnp.sum(w0.reshape(-1, sub, width), axis=0)
            part1 = part1 + jnp.sum(w1.reshape(-1, sub, width), axis=0)
        l_scr[...] = l_new
        acc0_scr[...] = acc0_scr[...] * expand(alpha, e0f_ref) + part0
        acc1_scr[...] = acc1_scr[...] * expand(alpha, e1f_ref) + part1

    s_list, v_list = [], []
    for i in range(n_pages):
        s = scores(k_refs[i][0])
        pos = (pg * n_pages + i) * PAGE_SIZE + lax.broadcasted_iota(jnp.int32, s.shape, 0)
        s_list.append(s - slope * (past_len - pos).astype(f32))
        v_list.append(v_refs[i][0])
    accumulate(s_list, v_list)

    @pl.when(pg == pl.num_programs(1) - 1)
    def _():
        k_new = jnp.broadcast_to(kn_ref[0], (sub, width))
        v_new = jnp.broadcast_to(vn_ref[0], (sub, width))
        s = scores(k_new)
        s = jnp.where(lax.broadcasted_iota(jnp.int32, s.shape, 0) == 0, s, MASKED)
        accumulate([s], [v_new])
        lam = _lambda(lp_ref[...], lam_init)
        inv0 = expand(1.0 / l_scr[...], e0f_ref)
        inv1 = expand(1.0 / l_scr[...], e1f_ref)
        o = (jnp.sum(acc0_scr[...], axis=0, keepdims=True) * inv0
             - lam * (jnp.sum(acc1_scr[...], axis=0, keepdims=True) * inv1))
        for h in range(width // LANES):
            cols = slice(h * LANES, (h + 1) * LANES)
            o_ref[0, :, cols] = _sub_norm(o[:, cols], gs_ref[...], lam_init).astype(o_ref.dtype)


def _decode_attention(q, k_new, v_new, cache_k, cache_v, page_table, slopes_np, lam_params, gsub, lam_init,
                      n_heads, pages_per_step):
    n_b, _, width = q.shape
    n_pg = page_table.shape[1]
    past_len = n_pg * PAGE_SIZE
    head_w = width // n_heads
    n_maps = 2 * n_heads
    lane = np.arange(width)
    seg = (lane[:, None] // (head_w // 2) == np.arange(n_maps)[None, :]).astype(np.float32)
    e0 = (2 * (lane[None, :] // head_w) == np.arange(n_maps)[:, None]).astype(np.float32)
    e1 = (2 * (lane[None, :] // head_w) + 1 == np.arange(n_maps)[:, None]).astype(np.float32)
    slope_maps = jnp.asarray(np.repeat(slopes_np, 2)[None, :], f32)
    pt_flat = page_table.reshape(-1)
    kpages = cache_k.reshape(cache_k.shape[0], PAGE_SIZE, width)
    vpages = cache_v.reshape(cache_v.shape[0], PAGE_SIZE, width)
    row = pl.BlockSpec((1, 1, width), lambda b, g, pt: (b, 0, 0))

    def page_spec(i):
        return pl.BlockSpec((1, PAGE_SIZE, width),
                            lambda b, g, pt: (pt[b * n_pg + g * pages_per_step + i], 0, 0))

    consts = [lam_params, gsub, slope_maps, jnp.asarray(seg, bf16), jnp.asarray(e0, bf16), jnp.asarray(e1, bf16),
              jnp.asarray(e0, f32), jnp.asarray(e1, f32)]
    grid_spec = pltpu.PrefetchScalarGridSpec(
        num_scalar_prefetch=1,
        grid=(n_b, n_pg // pages_per_step),
        in_specs=[pl.BlockSpec(a.shape, lambda b, g, pt, n=a.ndim: (0,) * n) for a in consts]
                 + [row, row, row]
                 + [page_spec(i) for i in range(pages_per_step)] * 2,
        out_specs=row,
        scratch_shapes=[pltpu.VMEM((1, n_maps), f32), pltpu.VMEM((1, n_maps), f32),
                        pltpu.VMEM((8, width), f32), pltpu.VMEM((8, width), f32)],
    )
    return pl.pallas_call(
        functools.partial(_decode_attn_kernel, lam_init, pages_per_step, past_len),
        out_shape=jax.ShapeDtypeStruct((n_b, 1, width), bf16),
        grid_spec=grid_spec,
        compiler_params=_params("arbitrary", "arbitrary"),
        name="decode_attention",
    )(pt_flat, *consts, q, k_new, v_new, *([kpages] * pages_per_step), *([vpages] * pages_per_step))


def _time_major(a):
    return jnp.swapaxes(a, 0, 1).reshape((a.shape[0] * a.shape[1],) + a.shape[2:])


def _batch_major(a, n_seq):
    return jnp.swapaxes(a.reshape((a.shape[0] // n_seq, n_seq) + a.shape[1:]), 0, 1)


def kernel(x_prompt, x_sample, c_prompt, c_sample, cache_k, cache_v, page_table, state_lru_h, state_lru_conv, state_ffn_conv, w_ada, b_ada, g_norm_mix, g_norm_ffn, w_lru_in, w_lru_conv, b_lru_conv, w_gate_x, b_gate_x, w_gate_a, b_gate_a, lru_log_param, w_lru_out, w_ada_kv, b_ada_kv, g_norm_kv, w_kv, g_k_norm, w_q, g_q_norm, lam_q1, lam_k1, lam_q2, lam_k2, g_subln, w_o, w_up, w_ffn_conv, b_ffn_conv, w_down):
    depth, d, _ = w_ada.shape
    n_a = w_lru_in.shape[0]
    n_bp, seq, _ = x_prompt.shape
    n_bs, dec_seq, _ = x_sample.shape
    assert dec_seq == 1 and n_bp == 8 and n_bs % 8 == 0
    d_lru = w_lru_out.shape[1]
    d_ff = w_down.shape[1]
    head_dim = g_q_norm.shape[1]
    assert 2 * head_dim == LANES
    attn_w = w_q.shape[2]
    n_heads = attn_w // LANES
    n_conv_lru = w_lru_conv.shape[1]
    n_conv_ffn = w_ffn_conv.shape[1]

    c_all = jnp.concatenate([c_sample, c_prompt, jnp.zeros((8, d), f32)], axis=0)
    mod = _ada(c_all, w_ada, b_ada.reshape(depth, 1, 6 * d), 1536)
    mod_kv = _ada(c_all, w_ada_kv[None], b_ada_kv.reshape(1, 1, 2 * d), 1024)
    prompt_rows = n_bs // n_bp

    win16 = w_lru_in.astype(bf16)
    wgx16 = w_gate_x.astype(bf16)
    wga16 = w_gate_a.astype(bf16)
    wout16 = w_lru_out.astype(bf16)
    wkv16 = w_kv.astype(bf16)
    wq16 = w_q.astype(bf16)
    wo16 = w_o.astype(bf16)
    wup16 = w_up.astype(bf16)
    wdown16 = w_down.astype(bf16)
    row2 = lambda a: a.reshape(1, -1)
    slopes_np = (2.0 ** (-8.0 * np.arange(1, n_heads + 1, dtype=np.float64) / n_heads)).astype(np.float32)
    slopes = jnp.asarray(slopes_np)

    def run_group(x, group, mod_rows, steps_lru, steps_ffn, steps_proj, lru_h0, lru_conv0, ffn_conv0, attend):
        lru_h, lru_conv, ffn_conv = [], [], []
        k_new = v_new = k16 = v16 = None
        for l in range(depth):
            if l < n_a:
                x, h_last, conv_new = _lru_layer(
                    x, mod, mod_rows, l, group, steps_lru, row2(g_norm_mix[l]), win16[l], w_lru_conv[l],
                    row2(b_lru_conv[l]), wgx16[l], row2(b_gate_x[l]), wga16[l], row2(b_gate_a[l]),
                    row2(lru_log_param[l]), wout16[l], lru_h0[l], lru_conv0[l])
                lru_h.append(h_last)
                lru_conv.append(conv_new)
            else:
                j = l - n_a
                lam_init = 0.8 - 0.6 * math.exp(-0.3 * l)
                lam_params = jnp.stack([lam_q1[j], lam_k1[j], lam_q2[j], lam_k2[j]])
                gq2 = jnp.tile(g_q_norm[j], 2)[None]
                attn = attend(x, l, wq16[j], gq2, lam_params, row2(g_subln[j]), lam_init, k_new, v_new, k16, v16)
                x = _o_proj(attn, x, mod, mod_rows, l, group, steps_proj, wo16[j])
            x, fb = _ffn_layer(x, mod, mod_rows, l, group, steps_ffn, 512, row2(g_norm_ffn[l]), wup16[l],
                               w_ffn_conv[l], row2(b_ffn_conv[l]), wdown16[l], ffn_conv0[l])
            ffn_conv.append(fb)
            if l == n_a - 1:
                k_new, v_new, k16, v16 = _kv_proj(x, mod_kv, mod_rows, group, steps_proj, row2(g_norm_kv), wkv16,
                                                  jnp.tile(g_k_norm, 2)[None])
        return x, k_new, v_new, lru_h, lru_conv, ffn_conv

    def attend_prompt(x, l, wq, gq2, lam_params, gsub, lam_init, k_new, v_new, k16, v16):
        q = _q_proj(x, mod, prompt_rows, l, n_bp, min(64, seq), row2(g_norm_mix[l]), wq, gq2, bf16)
        view = lambda a: a.reshape(seq, n_bp * attn_w)
        o = _prompt_attention(view(q), view(k16), view(v16), slopes, lam_params, gsub, lam_init,
                              n_bp, n_heads, min(512, seq), min(512, seq))
        return o.reshape(seq * n_bp, attn_w)

    xp = _time_major(x_prompt)
    zeros_p = lambda n, r, w: [jnp.zeros((r, w), f32)] * n
    yp, kp, vp, hp, lcp, fcp = run_group(
        xp, n_bp, prompt_rows, min(64, seq), min(128, seq), min(64, seq),
        zeros_p(n_a, n_bp, d_lru), zeros_p(n_a, (n_conv_lru - 1) * n_bp, d_lru),
        zeros_p(depth, (n_conv_ffn - 1) * n_bp, 2 * d_ff), attend_prompt)
    y_prompt = _batch_major(yp, n_bp)
    k_prompt = _batch_major(kp, n_bp).reshape(n_bp, seq, n_heads, 2, head_dim)
    v_prompt = _batch_major(vp, n_bp).reshape(n_bp, seq, n_heads, 2 * head_dim)
    state_bm = lambda s: jnp.stack([_batch_major(a, n_bp) for a in s])
    h_p = jnp.stack(hp)
    lc_p = state_bm(lcp)
    fc_p = state_bm(fcp)

    def attend_sample(x, l, wq, gq2, lam_params, gsub, lam_init, k_new, v_new, k16, v16):
        q = _q_proj(x, mod, 0, l, n_bs, 1, row2(g_norm_mix[l]), wq, gq2, f32)
        three = lambda a: a.reshape(n_bs, 1, attn_w)
        o = _decode_attention(three(q), three(k_new), three(v_new), cache_k, cache_v, page_table, slopes_np,
                              lam_params, gsub, lam_init, n_heads, min(8, page_table.shape[1]))
        return o.reshape(n_bs, attn_w)

    xs = x_sample.reshape(n_bs, d)
    ys, ks, vs, hs, lcs, fcs = run_group(
        xs, n_bs, 0, 1, 1, 1,
        [state_lru_h[l] for l in range(n_a)],
        [_time_major(state_lru_conv[l]) for l in range(n_a)],
        [_time_major(state_ffn_conv[l]) for l in range(depth)], attend_sample)
    y_sample = ys.reshape(n_bs, 1, d)
    k_sample = ks.reshape(n_bs, 1, n_heads, 2, head_dim)
    v_sample = vs.reshape(n_bs, 1, n_heads, 2 * head_dim)
    state_bs = lambda s: jnp.stack([_batch_major(a, n_bs) for a in s])
    h_s = jnp.stack(hs)
    lc_s = state_bs(lcs)
    fc_s = state_bs(fcs)

    return (y_prompt, y_sample, k_prompt, v_prompt, k_sample, v_sample,
            h_p, h_s, lc_p, lc_s, fc_p, fc_s)
```

```python
import functools
import math

import numpy as np
import jax
import jax.numpy as jnp
from jax import lax
from jax.experimental import pallas as pl
from jax.experimental.pallas import tpu as pltpu

f32 = jnp.float32
bf16 = jnp.bfloat16

EPS = 1e-6
LRU_C = 8.0
LANES = 128
PAGE_SIZE = 128
MASKED = -1e30
LOG2E = math.log2(math.e)
VMEM_LIMIT_BYTES = 56 * 1024 * 1024


def _params(*semantics):
    return pltpu.CompilerParams(dimension_semantics=semantics,
                                vmem_limit_bytes=VMEM_LIMIT_BYTES)


def _const_spec(shape):
    zeros = (0,) * len(shape)
    return pl.BlockSpec(shape, lambda *_: zeros)


def _layer_spec(stacked_shape, layer):
    index = (layer,) + (0,) * (len(stacked_shape) - 1)
    return pl.BlockSpec((None,) + tuple(stacked_shape[1:]), lambda *_: index)


def _sigmoid(x):
    return 1.0 / (1.0 + jnp.exp(-x))


def _gelu_tanh(x):
    return 0.5 * x * (1.0 + jnp.tanh(math.sqrt(2.0 / math.pi) * (x + 0.044715 * (x * x * x))))


def _mod_norm(x, gain, shift, scale, group):
    rows, d = x.shape
    y = x * lax.rsqrt(jnp.mean(x * x, axis=-1, keepdims=True) + EPS) * gain
    y = y.reshape(rows // group, group, d) * (1.0 + scale)[None] + shift[None]
    return y.reshape(rows, d)


def _gate(x, gate, group):
    rows, d = x.shape
    return (x.reshape(rows // group, group, d) * gate[None]).reshape(rows, d)


def _head_half_norm(y, gain2):
    lane = lax.broadcasted_iota(jnp.int32, y.shape, 1)
    low = lane < (LANES // 2)
    sq = y * y
    s_low = jnp.sum(jnp.where(low, sq, 0.0), axis=-1, keepdims=True)
    s_high = jnp.sum(jnp.where(low, 0.0, sq), axis=-1, keepdims=True)
    inv = lax.rsqrt(jnp.where(low, s_low, s_high) * (2.0 / LANES) + EPS)
    return y * inv * gain2


def _store_batch_major(o_ref, slabs_ref, group):
    steps = slabs_ref.shape[1] // group
    for g in range(group):
        for j in range(slabs_ref.shape[0]):
            o_ref[g, :, j * LANES:(j + 1) * LANES] = (
                slabs_ref.at[j][pl.ds(g, steps, stride=group), :].astype(o_ref.dtype))


def _load_step_major(a_ref, slabs_ref, group):
    steps = a_ref.shape[1]
    n_slabs = slabs_ref.shape[0]
    for g in range(group):
        for j in range(n_slabs):
            slabs_ref.at[j][pl.ds(g, steps, stride=group), :] = (
                a_ref[g, :, j * LANES:(j + 1) * LANES].astype(f32))
    return jnp.concatenate([slabs_ref[j] for j in range(n_slabs)], axis=-1)


def _ada_kernel(c_ref, w_ref, b_ref, o_ref):
    c = c_ref[...]
    sc = (c * _sigmoid(c)).astype(bf16)
    o_ref[0] = jnp.dot(sc, w_ref[0].astype(bf16), preferred_element_type=f32) + b_ref[0]


def _ada(c, w, b, tn):
    n_layers, d, n = w.shape
    m = c.shape[0]
    return pl.pallas_call(
        _ada_kernel,
        out_shape=jax.ShapeDtypeStruct((n_layers, m, n), f32),
        grid=(n_layers, n // tn),
        in_specs=[_const_spec((m, d)),
                  pl.BlockSpec((1, d, tn), lambda l, j: (l, 0, j)),
                  pl.BlockSpec((1, 1, tn), lambda l, j: (l, 0, j))],
        out_specs=pl.BlockSpec((1, m, tn), lambda l, j: (l, 0, j)),
        compiler_params=_params("arbitrary", "arbitrary"),
        name="ada_mod",
    )(c, w, b)


def _mod_spec(group, row_block, layer, piece, d):
    return pl.BlockSpec((1, group, d), lambda *_: (layer, row_block, piece))


def _lru_kernel(group, steps, n_conv,
                x_ref, sh_ref, sc_ref, gt_ref, gn_ref, win_ref, cw_ref, cb_ref,
                wgx_ref, bgx_ref, wga_ref, bga_ref, lp_ref, wout_ref, h0_ref, c0_ref,
                o_ref, hlast_ref, cnew_ref,
                xw_scr, halo_scr, a_scr, b_scr, gb_scr, y_scr, h_scr):
    rows = group * steps
    d_lru = a_scr.shape[1]
    halo = (n_conv - 1) * group
    t = pl.program_id(0)

    @pl.when(t == 0)
    def _():
        h_scr[...] = h0_ref[...]
        halo_scr[...] = c0_ref[...]

    x = x_ref[...]
    h = _mod_norm(x, gn_ref[...], sh_ref[0], sc_ref[0], group).astype(bf16)
    u = jnp.dot(h, win_ref[...], preferred_element_type=f32)
    gb_scr[...] = u[:, d_lru:]

    xw_scr[0:halo, :] = halo_scr[...]
    xw_scr[halo:halo + rows, :] = u[:, :d_lru]
    xc = cb_ref[...] + cw_ref[0:1, :] * xw_scr[0:rows, :]
    for j in range(1, n_conv):
        xc = xc + cw_ref[j:j + 1, :] * xw_scr[j * group:j * group + rows, :]
    new_halo = xw_scr[rows:rows + halo, :]
    halo_scr[...] = new_halo
    cnew_ref[...] = new_halo

    lp = lp_ref[...]
    log_sig = jnp.minimum(lp, 0.0) - jnp.log1p(jnp.exp(-jnp.abs(lp)))
    n_blocks = wgx_ref.shape[0]
    blk = d_lru // n_blocks
    for n in range(n_blocks):
        cols = slice(n * blk, (n + 1) * blk)
        xn = xc[:, cols]
        xn16 = xn.astype(bf16)
        gate_x = _sigmoid(jnp.dot(xn16, wgx_ref[n], preferred_element_type=f32) + bgx_ref[:, cols])
        gate_a = _sigmoid(jnp.dot(xn16, wga_ref[n], preferred_element_type=f32) + bga_ref[:, cols])
        log_a = LRU_C * gate_a * log_sig[:, cols]
        a = jnp.exp(log_a)
        a_scr[:, cols] = a
        b_scr[:, cols] = jnp.sqrt(-jnp.tanh(log_a) * (a * a + 1.0)) * (gate_x * xn)

    def step(i, hprev):
        r = pl.multiple_of(i * group, group)
        hnew = a_scr[pl.ds(r, group), :] * hprev + b_scr[pl.ds(r, group), :]
        b_scr[pl.ds(r, group), :] = hnew
        return hnew

    if steps == 1:
        hlast = step(0, h_scr[...])
    else:
        hlast = lax.fori_loop(0, steps, step, h_scr[...], unroll=8)
    h_scr[...] = hlast
    hlast_ref[...] = hlast

    y_scr[...] = (b_scr[...] * _gelu_tanh(gb_scr[...])).astype(bf16)
    out = jnp.dot(y_scr[...], wout_ref[...], preferred_element_type=f32)
    o_ref[...] = x + _gate(out, gt_ref[0], group)


def _lru_layer(x, mod, mod_rows, layer, group, steps, gn, win, cw, cb, wgx, bgx, wga, bga, lp, wout, h0, c0):
    rows_total, d = x.shape
    d_lru = wout.shape[1]
    n_conv = cw.shape[1]
    rows = group * steps
    halo = (n_conv - 1) * group
    spec = lambda p: _mod_spec(group, mod_rows, layer, p, d)
    par = lambda a: _layer_spec(a.shape, layer)
    tile = pl.BlockSpec((rows, d), lambda t: (t, 0))
    kernel = functools.partial(_lru_kernel, group, steps, n_conv)
    return pl.pallas_call(
        kernel,
        out_shape=(jax.ShapeDtypeStruct((rows_total, d), f32),
                   jax.ShapeDtypeStruct((group, d_lru), f32),
                   jax.ShapeDtypeStruct((halo, d_lru), f32)),
        grid=(rows_total // rows,),
        in_specs=[tile, spec(0), spec(1), spec(2), par(gn), par(win), par(cw), par(cb), par(wgx), par(bgx),
                  par(wga), par(bga), par(lp), par(wout), _const_spec(h0.shape), _const_spec(c0.shape)],
        out_specs=(tile, _const_spec((group, d_lru)), _const_spec((halo, d_lru))),
        scratch_shapes=[pltpu.VMEM((rows + halo, d_lru), f32), pltpu.VMEM((halo, d_lru), f32),
                        pltpu.VMEM((rows, d_lru), f32), pltpu.VMEM((rows, d_lru), f32),
                        pltpu.VMEM((rows, d_lru), f32), pltpu.VMEM((rows, d_lru), bf16),
                        pltpu.VMEM((group, d_lru), f32)],
        compiler_params=_params("arbitrary"),
        name="rglru_layer",
    )(x, mod, mod, mod, gn, win, cw, cb, wgx, bgx, wga, bga, lp, wout, h0, c0)


def _ffn_kernel(group, steps, n_conv,
                x_ref, sh_ref, sc_ref, gt_ref, gn_ref, wa_ref, wg_ref, cwa_ref, cwg_ref, cba_ref, cbg_ref,
                wd_ref, c0a_ref, c0g_ref,
                o_ref, cna_ref, cng_ref,
                h_scr, acc_scr, uwa_scr, uwg_scr, haloa_scr, halog_scr):
    rows = group * steps
    halo = (n_conv - 1) * group
    t = pl.program_id(0)
    c = pl.program_id(1)

    @pl.when(c == 0)
    def _():
        h_scr[...] = _mod_norm(x_ref[...], gn_ref[...], sh_ref[0], sc_ref[0], group).astype(bf16)
        acc_scr[...] = jnp.zeros_like(acc_scr)

    @pl.when(t == 0)
    def _():
        haloa_scr[c] = c0a_ref[...]
        halog_scr[c] = c0g_ref[...]

    h = h_scr[...]

    def conv_half(w_ref, cw_ref, cb_ref, uw_scr, halo_scr, cn_ref):
        uw_scr[0:halo, :] = halo_scr[c]
        uw_scr[halo:halo + rows, :] = jnp.dot(h, w_ref[...], preferred_element_type=f32)
        y = cb_ref[...] + cw_ref[0:1, :] * uw_scr[0:rows, :]
        for j in range(1, n_conv):
            y = y + cw_ref[j:j + 1, :] * uw_scr[j * group:j * group + rows, :]
        new_halo = uw_scr[rows:rows + halo, :]
        halo_scr[c] = new_halo
        cn_ref[...] = new_halo
        return y

    a = conv_half(wa_ref, cwa_ref, cba_ref, uwa_scr, haloa_scr, cna_ref)
    g = conv_half(wg_ref, cwg_ref, cbg_ref, uwg_scr, halog_scr, cng_ref)
    act = (_gelu_tanh(a) * g).astype(bf16)
    acc_scr[...] += jnp.dot(act, wd_ref[...], preferred_element_type=f32)

    @pl.when(c == pl.num_programs(1) - 1)
    def _():
        o_ref[...] = x_ref[...] + _gate(acc_scr[...], gt_ref[0], group)


def _ffn_layer(x, mod, mod_rows, layer, group, steps, fc, gn, wup, cw, cb, wdown, c0):
    rows_total, d = x.shape
    d_ff = wdown.shape[1]
    n_conv = cw.shape[1]
    rows = group * steps
    halo = (n_conv - 1) * group
    nc = d_ff // fc
    nt = rows_total // rows
    spec = lambda p: _mod_spec(group, mod_rows, layer, p, d)
    tile = pl.BlockSpec((rows, d), lambda t, c: (t, 0))
    col_a = lambda r: pl.BlockSpec((None, r, fc), lambda t, c: (layer, 0, c))
    col_g = lambda r: pl.BlockSpec((None, r, fc), lambda t, c: (layer, 0, nc + c))
    state_out = pl.BlockSpec((halo, fc), lambda t, c: (0, jnp.where(t == nt - 1, c, 0)))
    kernel = functools.partial(_ffn_kernel, group, steps, n_conv)
    out, cna, cng = pl.pallas_call(
        kernel,
        out_shape=(jax.ShapeDtypeStruct((rows_total, d), f32),
                   jax.ShapeDtypeStruct((halo, d_ff), f32),
                   jax.ShapeDtypeStruct((halo, d_ff), f32)),
        grid=(nt, nc),
        in_specs=[tile, spec(3), spec(4), spec(5), _layer_spec(gn.shape, layer),
                  col_a(d), col_g(d), col_a(n_conv), col_g(n_conv), col_a(1), col_g(1),
                  pl.BlockSpec((None, fc, d), lambda t, c: (layer, c, 0)),
                  pl.BlockSpec((halo, fc), lambda t, c: (0, c)),
                  pl.BlockSpec((halo, fc), lambda t, c: (0, nc + c))],
        out_specs=(tile, state_out, state_out),
        scratch_shapes=[pltpu.VMEM((rows, d), bf16), pltpu.VMEM((rows, d), f32),
                        pltpu.VMEM((rows + halo, fc), f32), pltpu.VMEM((rows + halo, fc), f32),
                        pltpu.VMEM((nc, halo, fc), f32), pltpu.VMEM((nc, halo, fc), f32)],
        compiler_params=_params("arbitrary", "arbitrary"),
        name="conv_ffn",
    )(x, mod, mod, mod, gn, wup, wup, cw, cw, cb, cb, wdown, c0, c0)
    return out, jnp.concatenate([cna, cng], axis=-1)


def _q_kernel(group, scale, batch_major, x_ref, sh_ref, sc_ref, gn_ref, w_ref, gq_ref, o_ref, *slabs):
    h = _mod_norm(x_ref[...], gn_ref[...], sh_ref[0], sc_ref[0], group).astype(bf16)
    q = jnp.dot(h, w_ref[...], preferred_element_type=f32)
    for j in range(q.shape[1] // LANES):
        cols = slice(j * LANES, (j + 1) * LANES)
        qj = _head_half_norm(q[:, cols], gq_ref[...]) * scale
        if batch_major:
            slabs[0][j] = qj
        else:
            o_ref[:, cols] = qj.astype(o_ref.dtype)
    if batch_major:
        _store_batch_major(o_ref, slabs[0], group)


def _q_proj(x, mod, mod_rows, layer, block, group, steps, gn, wq, gq2, scale, out_dtype, batch_major):
    rows_total, d = x.shape
    width = wq.shape[2]
    rows = group * steps
    spec = lambda p: _mod_spec(group, mod_rows, layer, p, d)
    if batch_major:
        out_shape = jax.ShapeDtypeStruct((group, rows_total // group, width), out_dtype)
        out_spec = pl.BlockSpec((group, steps, width), lambda t: (0, t, 0))
        scratch = [pltpu.VMEM((width // LANES, rows, LANES), f32)]
    else:
        out_shape = jax.ShapeDtypeStruct((rows_total, width), out_dtype)
        out_spec = pl.BlockSpec((rows, width), lambda t: (t, 0))
        scratch = []
    return pl.pallas_call(
        functools.partial(_q_kernel, group, scale, batch_major),
        out_shape=out_shape,
        grid=(rows_total // rows,),
        in_specs=[pl.BlockSpec((rows, d), lambda t: (t, 0)), spec(0), spec(1), _layer_spec(gn.shape, layer),
                  _layer_spec(wq.shape, block), _const_spec(gq2.shape)],
        out_specs=out_spec,
        scratch_shapes=scratch,
        compiler_params=_params("arbitrary"),
        name="q_proj",
    )(x, mod, mod, gn, wq, gq2)


def _kv_kernel(group, batch_major, x_ref, sh_ref, sc_ref, gn_ref, w_ref, gk_ref,
               k_ref, v_ref, k16_ref, v16_ref, *slabs):
    h = _mod_norm(x_ref[...], gn_ref[...], sh_ref[0], sc_ref[0], group).astype(bf16)
    kv = jnp.dot(h, w_ref[...], preferred_element_type=f32)
    width = kv.shape[1] // 2
    for j in range(width // LANES):
        cols = slice(j * LANES, (j + 1) * LANES)
        kj = _head_half_norm(kv[:, cols], gk_ref[...])
        vj = kv[:, width + j * LANES:width + (j + 1) * LANES]
        if batch_major:
            slabs[0][j] = kj
            slabs[1][j] = vj
        else:
            k_ref[:, cols] = kj
            k16_ref[:, cols] = kj.astype(bf16)
            v_ref[:, cols] = vj
            v16_ref[:, cols] = vj.astype(bf16)
    if batch_major:
        _store_batch_major(k_ref, slabs[0], group)
        _store_batch_major(k16_ref, slabs[0], group)
        _store_batch_major(v_ref, slabs[1], group)
        _store_batch_major(v16_ref, slabs[1], group)


def _kv_proj(x, mod_kv, mod_rows, group, steps, gn, wkv, gk2, batch_major):
    rows_total, d = x.shape
    width = wkv.shape[1] // 2
    rows = group * steps
    spec = lambda p: _mod_spec(group, mod_rows, 0, p, d)
    if batch_major:
        shape = (group, rows_total // group, width)
        out_tile = pl.BlockSpec((group, steps, width), lambda t: (0, t, 0))
        scratch = [pltpu.VMEM((width // LANES, rows, LANES), f32)] * 2
    else:
        shape = (rows_total, width)
        out_tile = pl.BlockSpec((rows, width), lambda t: (t, 0))
        scratch = []
    return pl.pallas_call(
        functools.partial(_kv_kernel, group, batch_major),
        out_shape=(jax.ShapeDtypeStruct(shape, f32), jax.ShapeDtypeStruct(shape, f32),
                   jax.ShapeDtypeStruct(shape, bf16), jax.ShapeDtypeStruct(shape, bf16)),
        grid=(rows_total // rows,),
        in_specs=[pl.BlockSpec((rows, d), lambda t: (t, 0)), spec(0), spec(1), _const_spec(gn.shape),
                  _const_spec(wkv.shape), _const_spec(gk2.shape)],
        out_specs=(out_tile, out_tile, out_tile, out_tile),
        scratch_shapes=scratch,
        compiler_params=_params("arbitrary"),
        name="kv_proj",
    )(x, mod_kv, mod_kv, gn, wkv, gk2)


def _o_kernel(group, batch_major, a_ref, x_ref, gt_ref, w_ref, o_ref, *slabs):
    if batch_major:
        a = _load_step_major(a_ref, slabs[0], group).astype(bf16)
    else:
        a = a_ref[...]
    out = jnp.dot(a, w_ref[...], preferred_element_type=f32)
    o_ref[...] = x_ref[...] + _gate(out, gt_ref[0], group)


def _o_proj(attn, x, mod, mod_rows, layer, block, group, steps, wo, batch_major):
    rows_total, d = x.shape
    width = wo.shape[1]
    rows = group * steps
    tile = pl.BlockSpec((rows, d), lambda t: (t, 0))
    if batch_major:
        a_spec = pl.BlockSpec((group, steps, width), lambda t: (0, t, 0))
        scratch = [pltpu.VMEM((width // LANES, rows, LANES), f32)]
    else:
        a_spec = pl.BlockSpec((rows, width), lambda t: (t, 0))
        scratch = []
    return pl.pallas_call(
        functools.partial(_o_kernel, group, batch_major),
        out_shape=jax.ShapeDtypeStruct((rows_total, d), f32),
        grid=(rows_total // rows,),
        in_specs=[a_spec, tile, _mod_spec(group, mod_rows, layer, 2, d), _layer_spec(wo.shape, block)],
        out_specs=tile,
        scratch_shapes=scratch,
        compiler_params=_params("arbitrary"),
        name="o_proj",
    )(attn, x, mod, wo)


def _lambda(lp, lam_init):
    s1 = jnp.sum(lp[0:1] * lp[1:2], axis=-1, keepdims=True)
    s2 = jnp.sum(lp[2:3] * lp[3:4], axis=-1, keepdims=True)
    return jnp.exp(s1) - jnp.exp(s2) + lam_init


def _sub_norm(o, gain, lam_init):
    return o * lax.rsqrt(jnp.mean(o * o, axis=-1, keepdims=True) + EPS) * gain * (1.0 - lam_init)


def _prompt_attn_kernel(lam_init, tile, slopes_ref, lp_ref, gs_ref, q_ref, k_ref, v_ref, o_ref,
                        bias_scr, diag_scr):
    head = pl.program_id(0)
    seq = pl.program_id(1)
    qi = pl.program_id(2)
    slope2 = slopes_ref[head] * LOG2E

    @pl.when((seq == 0) & (qi == 0))
    def _():
        dist = (lax.broadcasted_iota(jnp.int32, (tile, tile), 0)
                - lax.broadcasted_iota(jnp.int32, (tile, tile), 1))
        bias = -slope2 * dist.astype(f32)
        bias_scr[...] = bias
        diag_scr[...] = jnp.where(dist >= 0, bias, MASKED)

    q = q_ref[...]
    lane = lax.broadcasted_iota(jnp.int32, q.shape, 1)
    zero = jnp.zeros_like(q)
    q_maps = (jnp.where(lane < LANES // 2, q, zero), jnp.where(lane < LANES // 2, zero, q))

    def one_tile(j, carry, bias_ref, offset):
        start = pl.multiple_of(j * tile, tile)
        kt = k_ref[pl.ds(start, tile), :]
        vt = v_ref[pl.ds(start, tile), :]
        new = []
        for c in range(2):
            m, l, acc = carry[c]
            s = lax.dot_general(q_maps[c], kt, (((1,), (1,)), ((), ())), preferred_element_type=f32)
            s = s + bias_ref[...]
            m_new = jnp.maximum(m, jnp.max(s, axis=-1, keepdims=True) + offset)
            alpha = jnp.exp2(m - m_new)
            p = jnp.exp2(s - (m_new - offset))
            p_lanes = p[:, 0:LANES]
            for i in range(1, tile // LANES):
                p_lanes = p_lanes + p[:, i * LANES:(i + 1) * LANES]
            l_new = alpha * l + p_lanes
            acc_new = alpha * acc + jnp.dot(p.astype(bf16), vt, preferred_element_type=f32)
            new.append((m_new, l_new, acc_new))
        return tuple(new)

    def full_tile(j, carry):
        offset = -slope2 * ((qi - j) * tile).astype(f32)
        return one_tile(j, carry, bias_scr, offset)

    init = tuple((jnp.full((tile, 1), MASKED, f32), jnp.zeros((tile, LANES), f32), jnp.zeros((tile, LANES), f32))
                 for _ in range(2))
    carry = lax.fori_loop(0, qi, full_tile, init)
    (m0, l0, acc0), (m1, l1, acc1) = one_tile(qi, carry, diag_scr, 0.0)
    lam = _lambda(lp_ref[...], lam_init)
    o = (acc0 / jnp.sum(l0, axis=-1, keepdims=True)
         - lam * (acc1 / jnp.sum(l1, axis=-1, keepdims=True)))
    o_ref[...] = _sub_norm(o, gs_ref[...], lam_init).astype(o_ref.dtype)


def _prompt_attention(q, k, v, slopes, lam_params, gsub, lam_init, n_heads, tile):
    n_seq, t_len, _ = q.shape
    q_spec = pl.BlockSpec((None, tile, LANES), lambda h, g, i: (g, i, h))
    kv_spec = pl.BlockSpec((None, t_len, LANES), lambda h, g, i: (g, 0, h))
    return pl.pallas_call(
        functools.partial(_prompt_attn_kernel, lam_init, tile),
        out_shape=jax.ShapeDtypeStruct(q.shape, bf16),
        grid=(n_heads, n_seq, t_len // tile),
        in_specs=[pl.BlockSpec(memory_space=pltpu.SMEM), _const_spec(lam_params.shape), _const_spec(gsub.shape),
                  q_spec, kv_spec, kv_spec],
        out_specs=q_spec,
        scratch_shapes=[pltpu.VMEM((tile, tile), f32), pltpu.VMEM((tile, tile), f32)],
        compiler_params=_params("arbitrary", "arbitrary", "arbitrary"),
        name="prompt_attention",
    )(slopes, lam_params, gsub, q, k, v)


def _decode_attn_kernel(lam_init, n_pages, n_key_steps, past_len,
                        pt_ref, lp_ref, gs_ref, slope_ref, qt_ref, q_ref, kn_ref, vn_ref, *refs):
    k_refs = refs[:n_pages]
    v_refs = refs[n_pages:2 * n_pages]
    o_ref = refs[2 * n_pages]
    qcol_scr, s_scr, w_scr, wself_scr, acc_scr = refs[2 * n_pages + 1:]
    b = pl.program_id(0)
    g = pl.program_id(1)
    _, n_heads, n_maps, head_dim, _ = k_refs[0].shape

    @pl.when(g == 0)
    def _():
        lane = lax.broadcasted_iota(jnp.int32, qt_ref.shape, 1)
        col = jnp.sum(jnp.where(lane == b, qt_ref[...], 0.0), axis=-1, keepdims=True)
        qcol_scr[...] = jnp.broadcast_to(col, qcol_scr.shape)

    @pl.when(g < n_key_steps)
    def _():
        for i in range(n_pages):
            start = pl.multiple_of((g * n_pages + i) * PAGE_SIZE, PAGE_SIZE)
            for h in range(n_heads):
                for c in range(n_maps):
                    r = (h * n_maps + c) * head_dim
                    prod = k_refs[i][0, h, c] * qcol_scr[r:r + head_dim, :]
                    s_scr[pl.ds(c * n_heads + h, 1), pl.ds(start, PAGE_SIZE)] = (
                        jnp.sum(prod, axis=0, keepdims=True))

    @pl.when(g == n_key_steps - 1)
    def _():
        s = s_scr[...]
        pos = lax.broadcasted_iota(jnp.int32, s.shape, 1)
        s = s - slope_ref[...] * (past_len - pos).astype(f32)
        prod = q_ref[0] * kn_ref[0]
        low = lax.broadcasted_iota(jnp.int32, prod.shape, 1) < head_dim
        s_self = jnp.concatenate([jnp.sum(jnp.where(low, prod, 0.0), axis=-1, keepdims=True),
                                  jnp.sum(jnp.where(low, 0.0, prod), axis=-1, keepdims=True)], axis=0)
        m = jnp.maximum(jnp.max(s, axis=-1, keepdims=True), s_self)
        p = jnp.exp(s - m)
        p_self = jnp.exp(s_self - m)
        inv = 1.0 / (jnp.sum(p, axis=-1, keepdims=True) + p_self)
        lam = _lambda(lp_ref[...], lam_init)
        w_scr[...] = p[:n_heads] * inv[:n_heads] - lam * (p[n_heads:] * inv[n_heads:])
        w_self = p_self[:n_heads] * inv[:n_heads] - lam * (p_self[n_heads:] * inv[n_heads:])
        wself_scr[...] = jnp.broadcast_to(w_self, wself_scr.shape)
        acc_scr[...] = jnp.zeros_like(acc_scr)

    @pl.when(g >= n_key_steps)
    def _():
        lane = lax.broadcasted_iota(jnp.int32, (n_heads, PAGE_SIZE), 1)
        ones = jnp.ones((PAGE_SIZE, LANES), bf16)
        acc = jnp.zeros(acc_scr.shape, f32)
        for i in range(n_pages):
            start = pl.multiple_of(((g - n_key_steps) * n_pages + i) * PAGE_SIZE, PAGE_SIZE)
            w_page = w_scr[:, pl.ds(start, PAGE_SIZE)]
            one_token = jnp.concatenate([jnp.where(lane == t, w_page, 0.0) for t in range(PAGE_SIZE)], axis=0)
            spread = jnp.dot(one_token.astype(bf16), ones, preferred_element_type=f32)
            acc = acc + jnp.sum(spread.reshape(PAGE_SIZE, n_heads, LANES) * v_refs[i][...], axis=0)
        acc_scr[...] += acc

    @pl.when(g == pl.num_programs(1) - 1)
    def _():
        o = acc_scr[...] + wself_scr[...] * vn_ref[0]
        o_ref[0] = _sub_norm(o, gs_ref[...], lam_init).astype(o_ref.dtype)


def _decode_attention(q, k_new, v_new, cache_k, cache_v, page_table, slopes_np, lam_params, gsub, lam_init,
                      pages_per_step):
    n_b, width = q.shape
    n_pool, page, n_heads, n_maps, head_dim = cache_k.shape
    n_pg = page_table.shape[1]
    past_len = n_pg * page
    n_key_steps = n_pg // pages_per_step
    head_w = n_maps * head_dim
    k_pages = jnp.transpose(cache_k, (0, 2, 3, 4, 1))
    v_tokens = cache_v.reshape(n_pool * page, n_heads, head_w)
    heads = lambda a: a.reshape(n_b, n_heads, head_w)
    slope_rows = jnp.asarray(np.tile(slopes_np, n_maps)[:, None], f32)
    pt_flat = page_table.reshape(-1)
    consts = [lam_params, gsub, slope_rows, q.T]
    sample = pl.BlockSpec((1, n_heads, head_w), lambda b, g, pt: (b, 0, 0))

    def key_page(i):
        def index(b, g, pt):
            step = jnp.minimum(g, n_key_steps - 1)
            return (pt[b * n_pg + step * pages_per_step + i], 0, 0, 0, 0)
        return pl.BlockSpec((1, n_heads, n_maps, head_dim, page), index)

    def value_page(i):
        def index(b, g, pt):
            step = jnp.maximum(g - n_key_steps, 0)
            return (pt[b * n_pg + step * pages_per_step + i], 0, 0)
        return pl.BlockSpec((page, n_heads, head_w), index)

    grid_spec = pltpu.PrefetchScalarGridSpec(
        num_scalar_prefetch=1,
        grid=(n_b, 2 * n_key_steps),
        in_specs=[pl.BlockSpec(a.shape, lambda b, g, pt, n=a.ndim: (0,) * n) for a in consts]
                 + [sample, sample, sample]
                 + [key_page(i) for i in range(pages_per_step)]
                 + [value_page(i) for i in range(pages_per_step)],
        out_specs=sample,
        scratch_shapes=[pltpu.VMEM((width, LANES), f32),
                        pltpu.VMEM((n_maps * n_heads, past_len), f32),
                        pltpu.VMEM((n_heads, past_len), f32),
                        pltpu.VMEM((n_heads, head_w), f32),
                        pltpu.VMEM((n_heads, head_w), f32)],
    )
    return pl.pallas_call(
        functools.partial(_decode_attn_kernel, lam_init, pages_per_step, n_key_steps, past_len),
        out_shape=jax.ShapeDtypeStruct((n_b, n_heads, head_w), bf16),
        grid_spec=grid_spec,
        compiler_params=_params("arbitrary", "arbitrary"),
        name="decode_attention",
    )(pt_flat, *consts, heads(q), heads(k_new), heads(v_new),
      *([k_pages] * pages_per_step), *([v_tokens] * pages_per_step))


def _time_major(a):
    return jnp.swapaxes(a, 0, 1).reshape((a.shape[0] * a.shape[1],) + a.shape[2:])


def _batch_major(a, n_seq):
    return jnp.swapaxes(a.reshape((a.shape[0] // n_seq, n_seq) + a.shape[1:]), 0, 1)


def kernel(x_prompt, x_sample, c_prompt, c_sample, cache_k, cache_v, page_table, state_lru_h, state_lru_conv, state_ffn_conv, w_ada, b_ada, g_norm_mix, g_norm_ffn, w_lru_in, w_lru_conv, b_lru_conv, w_gate_x, b_gate_x, w_gate_a, b_gate_a, lru_log_param, w_lru_out, w_ada_kv, b_ada_kv, g_norm_kv, w_kv, g_k_norm, w_q, g_q_norm, lam_q1, lam_k1, lam_q2, lam_k2, g_subln, w_o, w_up, w_ffn_conv, b_ffn_conv, w_down):
    depth, d, _ = w_ada.shape
    n_a = w_lru_in.shape[0]
    n_bp, seq, _ = x_prompt.shape
    n_bs, dec_seq, _ = x_sample.shape
    assert dec_seq == 1 and n_bp == 8 and n_bs % 8 == 0 and n_bs <= LANES
    d_lru = w_lru_out.shape[1]
    d_ff = w_down.shape[1]
    head_dim = g_q_norm.shape[1]
    assert 2 * head_dim == LANES and cache_k.shape[1] == PAGE_SIZE
    attn_w = w_q.shape[2]
    n_heads = attn_w // LANES
    n_conv_lru = w_lru_conv.shape[1]
    n_conv_ffn = w_ffn_conv.shape[1]

    c_all = jnp.concatenate([c_sample, c_prompt, jnp.zeros((8, d), f32)], axis=0)
    mod = _ada(c_all, w_ada, b_ada.reshape(depth, 1, 6 * d), 1536)
    mod_kv = _ada(c_all, w_ada_kv[None], b_ada_kv.reshape(1, 1, 2 * d), 1024)
    prompt_rows = n_bs // n_bp

    win16 = w_lru_in.astype(bf16)
    wgx16 = w_gate_x.astype(bf16)
    wga16 = w_gate_a.astype(bf16)
    wout16 = w_lru_out.astype(bf16)
    wkv16 = w_kv.astype(bf16)
    wq16 = w_q.astype(bf16)
    wo16 = w_o.astype(bf16)
    wup16 = w_up.astype(bf16)
    wdown16 = w_down.astype(bf16)
    rows3 = lambda a: a[:, None, :]
    slopes_np = (2.0 ** (-8.0 * np.arange(1, n_heads + 1, dtype=np.float64) / n_heads)).astype(np.float32)
    slopes = jnp.asarray(slopes_np)
    head_scale = head_dim ** -0.5

    def run_group(x, group, mod_rows, steps_lru, steps_ffn, steps_proj, lru_h0, lru_conv0, ffn_conv0, attend,
                  batch_major):
        lru_h, lru_conv, ffn_conv = [], [], []
        kv = None
        for l in range(depth):
            if l < n_a:
                x, h_last, conv_new = _lru_layer(
                    x, mod, mod_rows, l, group, steps_lru, rows3(g_norm_mix), win16, w_lru_conv,
                    rows3(b_lru_conv), wgx16, rows3(b_gate_x), wga16, rows3(b_gate_a),
                    rows3(lru_log_param), wout16, lru_h0[l], lru_conv0[l])
                lru_h.append(h_last)
                lru_conv.append(conv_new)
            else:
                j = l - n_a
                lam_init = 0.8 - 0.6 * math.exp(-0.3 * l)
                lam_params = jnp.stack([lam_q1[j], lam_k1[j], lam_q2[j], lam_k2[j]])
                gq2 = jnp.tile(g_q_norm[j], 2)[None]
                attn = attend(x, l, j, gq2, lam_params, g_subln[j][None], lam_init, kv)
                x = _o_proj(attn, x, mod, mod_rows, l, j, group, steps_proj, wo16, batch_major)
            x, fb = _ffn_layer(x, mod, mod_rows, l, group, steps_ffn, 512, rows3(g_norm_ffn), wup16,
                               w_ffn_conv, rows3(b_ffn_conv), wdown16, ffn_conv0[l])
            ffn_conv.append(fb)
            if l == n_a - 1:
                kv = _kv_proj(x, mod_kv, mod_rows, group, steps_proj, g_norm_kv[None], wkv16,
                              jnp.tile(g_k_norm, 2)[None], batch_major)
        return x, kv[0], kv[1], lru_h, lru_conv, ffn_conv

    steps_proj_p = min(64, seq)

    def attend_prompt(x, l, j, gq2, lam_params, gsub, lam_init, kv):
        q = _q_proj(x, mod, prompt_rows, l, j, n_bp, steps_proj_p, rows3(g_norm_mix), wq16, gq2,
                    head_scale * LOG2E, bf16, True)
        return _prompt_attention(q, kv[2], kv[3], slopes, lam_params, gsub, lam_init, n_heads, min(512, seq))

    xp = _time_major(x_prompt)
    zeros_p = lambda n, r, w: [jnp.zeros((r, w), f32)] * n
    yp, kp, vp, hp, lcp, fcp = run_group(
        xp, n_bp, prompt_rows, min(64, seq), min(128, seq), steps_proj_p,
        zeros_p(n_a, n_bp, d_lru), zeros_p(n_a, (n_conv_lru - 1) * n_bp, d_lru),
        zeros_p(depth, (n_conv_ffn - 1) * n_bp, 2 * d_ff), attend_prompt, True)
    y_prompt = _batch_major(yp, n_bp)
    k_prompt = kp.reshape(n_bp, seq, n_heads, 2, head_dim)
    v_prompt = vp.reshape(n_bp, seq, n_heads, 2 * head_dim)
    state_bm = lambda s: jnp.stack([_batch_major(a, n_bp) for a in s])
    h_p = jnp.stack(hp)
    lc_p = state_bm(lcp)
    fc_p = state_bm(fcp)

    def attend_sample(x, l, j, gq2, lam_params, gsub, lam_init, kv):
        q = _q_proj(x, mod, 0, l, j, n_bs, 1, rows3(g_norm_mix), wq16, gq2, head_scale, f32, False)
        o = _decode_attention(q, kv[0], kv[1], cache_k, cache_v, page_table, slopes_np,
                              lam_params, gsub, lam_init, min(8, page_table.shape[1]))
        return o.reshape(n_bs, attn_w)

    xs = x_sample.reshape(n_bs, d)
    ys, ks, vs, hs, lcs, fcs = run_group(
        xs, n_bs, 0, 1, 1, 1,
        [state_lru_h[l] for l in range(n_a)],
        [_time_major(state_lru_conv[l]) for l in range(n_a)],
        [_time_major(state_ffn_conv[l]) for l in range(depth)], attend_sample, False)
    y_sample = ys.reshape(n_bs, 1, d)
    k_sample = ks.reshape(n_bs, 1, n_heads, 2, head_dim)
    v_sample = vs.reshape(n_bs, 1, n_heads, 2 * head_dim)
    state_bs = lambda s: jnp.stack([_batch_major(a, n_bs) for a in s])
    h_s = jnp.stack(hs)
    lc_s = state_bs(lcs)
    fc_s = state_bs(fcs)

    return (y_prompt, y_sample, k_prompt, v_prompt, k_sample, v_sample,
            h_p, h_s, lc_p, lc_s, fc_p, fc_s)
```

```python
import functools
import math

import numpy as np
import jax
import jax.numpy as jnp
from jax import lax
from jax.experimental import pallas as pl
from jax.experimental.pallas import tpu as pltpu

f32 = jnp.float32
bf16 = jnp.bfloat16

EPS = 1e-6
LRU_C = 8.0
LANES = 128
PAGE_SIZE = 128
MASKED = -1e30
LOG2E = math.log2(math.e)
VMEM_LIMIT_BYTES = 56 * 1024 * 1024


def _params(*semantics):
    return pltpu.CompilerParams(dimension_semantics=semantics,
                                vmem_limit_bytes=VMEM_LIMIT_BYTES)


def _const_spec(shape):
    zeros = (0,) * len(shape)
    return pl.BlockSpec(shape, lambda *_: zeros)


def _layer_spec(stacked_shape, layer):
    index = (layer,) + (0,) * (len(stacked_shape) - 1)
    return pl.BlockSpec((None,) + tuple(stacked_shape[1:]), lambda *_: index)


def _sigmoid(x):
    return 1.0 / (1.0 + jnp.exp(-x))


def _gelu_tanh(x):
    return 0.5 * x * (1.0 + jnp.tanh(math.sqrt(2.0 / math.pi) * (x + 0.044715 * (x * x * x))))


def _mod_norm(x, gain, shift, scale, group):
    rows, d = x.shape
    y = x * lax.rsqrt(jnp.mean(x * x, axis=-1, keepdims=True) + EPS) * gain
    y = y.reshape(rows // group, group, d) * (1.0 + scale)[None] + shift[None]
    return y.reshape(rows, d)


def _gate(x, gate, group):
    rows, d = x.shape
    return (x.reshape(rows // group, group, d) * gate[None]).reshape(rows, d)


def _head_half_norm(y, gain2):
    lane = lax.broadcasted_iota(jnp.int32, y.shape, 1)
    low = lane < (LANES // 2)
    sq = y * y
    s_low = jnp.sum(jnp.where(low, sq, 0.0), axis=-1, keepdims=True)
    s_high = jnp.sum(jnp.where(low, 0.0, sq), axis=-1, keepdims=True)
    inv = lax.rsqrt(jnp.where(low, s_low, s_high) * (2.0 / LANES) + EPS)
    return y * inv * gain2


def _store_batch_major(o_ref, slabs_ref, group):
    steps = slabs_ref.shape[1] // group
    for g in range(group):
        for j in range(slabs_ref.shape[0]):
            o_ref[g, :, j * LANES:(j + 1) * LANES] = (
                slabs_ref.at[j][pl.ds(g, steps, stride=group), :].astype(o_ref.dtype))


def _load_step_major(a_ref, slabs_ref, group):
    steps = a_ref.shape[1]
    n_slabs = slabs_ref.shape[0]
    for g in range(group):
        for j in range(n_slabs):
            slabs_ref.at[j][pl.ds(g, steps, stride=group), :] = (
                a_ref[g, :, j * LANES:(j + 1) * LANES].astype(f32))
    return jnp.concatenate([slabs_ref[j] for j in range(n_slabs)], axis=-1)


def _ada_kernel(c_ref, w_ref, b_ref, o_ref):
    c = c_ref[...]
    sc = (c * _sigmoid(c)).astype(bf16)
    o_ref[0] = jnp.dot(sc, w_ref[0].astype(bf16), preferred_element_type=f32) + b_ref[0]


def _ada(c, w, b, tn):
    n_layers, d, n = w.shape
    m = c.shape[0]
    return pl.pallas_call(
        _ada_kernel,
        out_shape=jax.ShapeDtypeStruct((n_layers, m, n), f32),
        grid=(n_layers, n // tn),
        in_specs=[_const_spec((m, d)),
                  pl.BlockSpec((1, d, tn), lambda l, j: (l, 0, j)),
                  pl.BlockSpec((1, 1, tn), lambda l, j: (l, 0, j))],
        out_specs=pl.BlockSpec((1, m, tn), lambda l, j: (l, 0, j)),
        compiler_params=_params("arbitrary", "arbitrary"),
        name="ada_mod",
    )(c, w, b)


def _mod_spec(group, row_block, layer, piece, d):
    return pl.BlockSpec((1, group, d), lambda *_: (layer, row_block, piece))


def _lru_kernel(group, steps, n_conv, batch_major_in,
                x_ref, sh_ref, sc_ref, gt_ref, gn_ref, win_ref, cw_ref, cb_ref,
                wg_ref, bgx_ref, bga_ref, lp_ref, wout_ref, h0_ref, c0_ref,
                o_ref, hlast_ref, cnew_ref,
                xw_scr, halo_scr, a_scr, b_scr, gb_scr, y_scr, h_scr, *slabs):
    rows = group * steps
    d_lru = a_scr.shape[1]
    halo = (n_conv - 1) * group
    t = pl.program_id(0)

    @pl.when(t == 0)
    def _():
        h_scr[...] = h0_ref[...]
        halo_scr[...] = c0_ref[...]

    x = _load_step_major(x_ref, slabs[0], group) if batch_major_in else x_ref[...]
    h = _mod_norm(x, gn_ref[...], sh_ref[0], sc_ref[0], group).astype(bf16)
    u = jnp.dot(h, win_ref[...], preferred_element_type=f32)
    gb_scr[...] = u[:, d_lru:]

    xw_scr[0:halo, :] = halo_scr[...]
    xw_scr[halo:halo + rows, :] = u[:, :d_lru]
    xc = cb_ref[...] + cw_ref[0:1, :] * xw_scr[0:rows, :]
    for j in range(1, n_conv):
        xc = xc + cw_ref[j:j + 1, :] * xw_scr[j * group:j * group + rows, :]
    new_halo = xw_scr[rows:rows + halo, :]
    halo_scr[...] = new_halo
    cnew_ref[...] = new_halo

    lp = lp_ref[...]
    log_sig = jnp.minimum(lp, 0.0) - jnp.log1p(jnp.exp(-jnp.abs(lp)))
    n_blocks = wg_ref.shape[0]
    blk = d_lru // n_blocks
    for n in range(n_blocks):
        cols = slice(n * blk, (n + 1) * blk)
        xn = xc[:, cols]
        gates = jnp.dot(xn.astype(bf16), wg_ref[n], preferred_element_type=f32)
        gate_x = _sigmoid(gates[:, :blk] + bgx_ref[:, cols])
        gate_a = _sigmoid(gates[:, blk:] + bga_ref[:, cols])
        log_a = LRU_C * gate_a * log_sig[:, cols]
        a = jnp.exp(log_a)
        a_scr[:, cols] = a
        b_scr[:, cols] = jnp.sqrt(-jnp.tanh(log_a) * (a * a + 1.0)) * (gate_x * xn)

    def step(i, hprev):
        r = pl.multiple_of(i * group, group)
        hnew = a_scr[pl.ds(r, group), :] * hprev + b_scr[pl.ds(r, group), :]
        b_scr[pl.ds(r, group), :] = hnew
        return hnew

    if steps == 1:
        hlast = step(0, h_scr[...])
    else:
        hlast = lax.fori_loop(0, steps, step, h_scr[...], unroll=8)
    h_scr[...] = hlast
    hlast_ref[...] = hlast

    y_scr[...] = (b_scr[...] * _gelu_tanh(gb_scr[...])).astype(bf16)
    out = jnp.dot(y_scr[...], wout_ref[...], preferred_element_type=f32)
    o_ref[...] = x + _gate(out, gt_ref[0], group)


def _lru_layer(x, mod, mod_rows, layer, group, steps, gn, win, cw, cb, wg, bgx, bga, lp, wout, h0, c0,
               batch_major_in):
    d = x.shape[-1]
    rows_total = x.shape[0] * x.shape[1] if batch_major_in else x.shape[0]
    d_lru = wout.shape[1]
    n_conv = cw.shape[1]
    rows = group * steps
    halo = (n_conv - 1) * group
    spec = lambda p: _mod_spec(group, mod_rows, layer, p, d)
    par = lambda a: _layer_spec(a.shape, layer)
    tile = pl.BlockSpec((rows, d), lambda t: (t, 0))
    if batch_major_in:
        x_spec = pl.BlockSpec((group, steps, d), lambda t: (0, t, 0))
        slabs = [pltpu.VMEM((d // LANES, rows, LANES), f32)]
    else:
        x_spec, slabs = tile, []
    kernel = functools.partial(_lru_kernel, group, steps, n_conv, batch_major_in)
    return pl.pallas_call(
        kernel,
        out_shape=(jax.ShapeDtypeStruct((rows_total, d), f32),
                   jax.ShapeDtypeStruct((group, d_lru), f32),
                   jax.ShapeDtypeStruct((halo, d_lru), f32)),
        grid=(rows_total // rows,),
        in_specs=[x_spec, spec(0), spec(1), spec(2), par(gn), par(win), par(cw), par(cb), par(wg), par(bgx),
                  par(bga), par(lp), par(wout), _const_spec(h0.shape), _const_spec(c0.shape)],
        out_specs=(tile, _const_spec((group, d_lru)), _const_spec((halo, d_lru))),
        scratch_shapes=[pltpu.VMEM((rows + halo, d_lru), f32), pltpu.VMEM((halo, d_lru), f32),
                        pltpu.VMEM((rows, d_lru), f32), pltpu.VMEM((rows, d_lru), f32),
                        pltpu.VMEM((rows, d_lru), f32), pltpu.VMEM((rows, d_lru), bf16),
                        pltpu.VMEM((group, d_lru), f32)] + slabs,
        compiler_params=_params("arbitrary"),
        name="rglru_layer",
    )(x, mod, mod, mod, gn, win, cw, cb, wg, bgx, bga, lp, wout, h0, c0)


def _ffn_kernel(group, steps, n_conv, batch_major_out,
                x_ref, sh_ref, sc_ref, gt_ref, gn_ref, wa_ref, wg_ref, cwa_ref, cwg_ref, cba_ref, cbg_ref,
                wd_ref, c0a_ref, c0g_ref,
                o_ref, cna_ref, cng_ref,
                h_scr, acc_scr, uwa_scr, uwg_scr, haloa_scr, halog_scr, *slabs):
    rows = group * steps
    halo = (n_conv - 1) * group
    t = pl.program_id(0)
    c = pl.program_id(1)

    @pl.when(c == 0)
    def _():
        h_scr[...] = _mod_norm(x_ref[...], gn_ref[...], sh_ref[0], sc_ref[0], group).astype(bf16)
        acc_scr[...] = jnp.zeros_like(acc_scr)

    @pl.when(t == 0)
    def _():
        haloa_scr[c] = c0a_ref[...]
        halog_scr[c] = c0g_ref[...]

    h = h_scr[...]

    def conv_half(w_ref, cw_ref, cb_ref, uw_scr, halo_scr, cn_ref):
        uw_scr[0:halo, :] = halo_scr[c]
        uw_scr[halo:halo + rows, :] = jnp.dot(h, w_ref[...], preferred_element_type=f32)
        y = cb_ref[...] + cw_ref[0:1, :] * uw_scr[0:rows, :]
        for j in range(1, n_conv):
            y = y + cw_ref[j:j + 1, :] * uw_scr[j * group:j * group + rows, :]
        new_halo = uw_scr[rows:rows + halo, :]
        halo_scr[c] = new_halo
        cn_ref[...] = new_halo
        return y

    a = conv_half(wa_ref, cwa_ref, cba_ref, uwa_scr, haloa_scr, cna_ref)
    g = conv_half(wg_ref, cwg_ref, cbg_ref, uwg_scr, halog_scr, cng_ref)
    act = (_gelu_tanh(a) * g).astype(bf16)
    acc_scr[...] += jnp.dot(act, wd_ref[...], preferred_element_type=f32)

    @pl.when(c == pl.num_programs(1) - 1)
    def _():
        y = x_ref[...] + _gate(acc_scr[...], gt_ref[0], group)
        if batch_major_out:
            for j in range(slabs[0].shape[0]):
                slabs[0][j] = y[:, j * LANES:(j + 1) * LANES]
            _store_batch_major(o_ref, slabs[0], group)
        else:
            o_ref[...] = y


def _ffn_layer(x, mod, mod_rows, layer, group, steps, fc, gn, wup, cw, cb, wdown, c0, batch_major_out):
    rows_total, d = x.shape
    d_ff = wdown.shape[1]
    n_conv = cw.shape[1]
    rows = group * steps
    halo = (n_conv - 1) * group
    nc = d_ff // fc
    nt = rows_total // rows
    spec = lambda p: _mod_spec(group, mod_rows, layer, p, d)
    tile = pl.BlockSpec((rows, d), lambda t, c: (t, 0))
    col_a = lambda r: pl.BlockSpec((None, r, fc), lambda t, c: (layer, 0, c))
    col_g = lambda r: pl.BlockSpec((None, r, fc), lambda t, c: (layer, 0, nc + c))
    state_out = pl.BlockSpec((halo, fc), lambda t, c: (0, jnp.where(t == nt - 1, c, 0)))
    if batch_major_out:
        out_shape = jax.ShapeDtypeStruct((group, rows_total // group, d), f32)
        out_spec = pl.BlockSpec((group, steps, d), lambda t, c: (0, t, 0))
        slabs = [pltpu.VMEM((d // LANES, rows, LANES), f32)]
    else:
        out_shape, out_spec, slabs = jax.ShapeDtypeStruct((rows_total, d), f32), tile, []
    kernel = functools.partial(_ffn_kernel, group, steps, n_conv, batch_major_out)
    out, cna, cng = pl.pallas_call(
        kernel,
        out_shape=(out_shape,
                   jax.ShapeDtypeStruct((halo, d_ff), f32),
                   jax.ShapeDtypeStruct((halo, d_ff), f32)),
        grid=(nt, nc),
        in_specs=[tile, spec(3), spec(4), spec(5), _layer_spec(gn.shape, layer),
                  col_a(d), col_g(d), col_a(n_conv), col_g(n_conv), col_a(1), col_g(1),
                  pl.BlockSpec((None, fc, d), lambda t, c: (layer, c, 0)),
                  pl.BlockSpec((halo, fc), lambda t, c: (0, c)),
                  pl.BlockSpec((halo, fc), lambda t, c: (0, nc + c))],
        out_specs=(out_spec, state_out, state_out),
        scratch_shapes=[pltpu.VMEM((rows, d), bf16), pltpu.VMEM((rows, d), f32),
                        pltpu.VMEM((rows + halo, fc), f32), pltpu.VMEM((rows + halo, fc), f32),
                        pltpu.VMEM((nc, halo, fc), f32), pltpu.VMEM((nc, halo, fc), f32)] + slabs,
        compiler_params=_params("arbitrary", "arbitrary"),
        name="conv_ffn",
    )(x, mod, mod, mod, gn, wup, wup, cw, cw, cb, cb, wdown, c0, c0)
    return out, jnp.concatenate([cna, cng], axis=-1)


def _q_kernel(group, scale, batch_major, x_ref, sh_ref, sc_ref, gn_ref, w_ref, gq_ref, o_ref, *slabs):
    h = _mod_norm(x_ref[...], gn_ref[...], sh_ref[0], sc_ref[0], group).astype(bf16)
    q = jnp.dot(h, w_ref[...], preferred_element_type=f32)
    for j in range(q.shape[1] // LANES):
        cols = slice(j * LANES, (j + 1) * LANES)
        qj = _head_half_norm(q[:, cols], gq_ref[...]) * scale
        if batch_major:
            slabs[0][j] = qj
        else:
            o_ref[:, cols] = qj.astype(o_ref.dtype)
    if batch_major:
        _store_batch_major(o_ref, slabs[0], group)


def _q_proj(x, mod, mod_rows, layer, block, group, steps, gn, wq, gq2, scale, out_dtype, batch_major):
    rows_total, d = x.shape
    width = wq.shape[2]
    rows = group * steps
    spec = lambda p: _mod_spec(group, mod_rows, layer, p, d)
    if batch_major:
        out_shape = jax.ShapeDtypeStruct((group, rows_total // group, width), out_dtype)
        out_spec = pl.BlockSpec((group, steps, width), lambda t: (0, t, 0))
        scratch = [pltpu.VMEM((width // LANES, rows, LANES), f32)]
    else:
        out_shape = jax.ShapeDtypeStruct((rows_total, width), out_dtype)
        out_spec = pl.BlockSpec((rows, width), lambda t: (t, 0))
        scratch = []
    return pl.pallas_call(
        functools.partial(_q_kernel, group, scale, batch_major),
        out_shape=out_shape,
        grid=(rows_total // rows,),
        in_specs=[pl.BlockSpec((rows, d), lambda t: (t, 0)), spec(0), spec(1), _layer_spec(gn.shape, layer),
                  _layer_spec(wq.shape, block), _const_spec(gq2.shape)],
        out_specs=out_spec,
        scratch_shapes=scratch,
        compiler_params=_params("arbitrary"),
        name="q_proj",
    )(x, mod, mod, gn, wq, gq2)


def _kv_kernel(group, batch_major, x_ref, sh_ref, sc_ref, gn_ref, w_ref, gk_ref,
               k_ref, v_ref, k16_ref, v16_ref, *slabs):
    h = _mod_norm(x_ref[...], gn_ref[...], sh_ref[0], sc_ref[0], group).astype(bf16)
    kv = jnp.dot(h, w_ref[...], preferred_element_type=f32)
    width = kv.shape[1] // 2
    for j in range(width // LANES):
        cols = slice(j * LANES, (j + 1) * LANES)
        kj = _head_half_norm(kv[:, cols], gk_ref[...])
        vj = kv[:, width + j * LANES:width + (j + 1) * LANES]
        if batch_major:
            slabs[0][j] = kj
            slabs[1][j] = vj
        else:
            k_ref[:, cols] = kj
            k16_ref[:, cols] = kj.astype(bf16)
            v_ref[:, cols] = vj
            v16_ref[:, cols] = vj.astype(bf16)
    if batch_major:
        _store_batch_major(k_ref, slabs[0], group)
        _store_batch_major(k16_ref, slabs[0], group)
        _store_batch_major(v_ref, slabs[1], group)
        _store_batch_major(v16_ref, slabs[1], group)


def _kv_proj(x, mod_kv, mod_rows, group, steps, gn, wkv, gk2, batch_major):
    rows_total, d = x.shape
    width = wkv.shape[1] // 2
    rows = group * steps
    spec = lambda p: _mod_spec(group, mod_rows, 0, p, d)
    if batch_major:
        shape = (group, rows_total // group, width)
        out_tile = pl.BlockSpec((group, steps, width), lambda t: (0, t, 0))
        scratch = [pltpu.VMEM((width // LANES, rows, LANES), f32)] * 2
    else:
        shape = (rows_total, width)
        out_tile = pl.BlockSpec((rows, width), lambda t: (t, 0))
        scratch = []
    return pl.pallas_call(
        functools.partial(_kv_kernel, group, batch_major),
        out_shape=(jax.ShapeDtypeStruct(shape, f32), jax.ShapeDtypeStruct(shape, f32),
                   jax.ShapeDtypeStruct(shape, bf16), jax.ShapeDtypeStruct(shape, bf16)),
        grid=(rows_total // rows,),
        in_specs=[pl.BlockSpec((rows, d), lambda t: (t, 0)), spec(0), spec(1), _const_spec(gn.shape),
                  _const_spec(wkv.shape), _const_spec(gk2.shape)],
        out_specs=(out_tile, out_tile, out_tile, out_tile),
        scratch_shapes=scratch,
        compiler_params=_params("arbitrary"),
        name="kv_proj",
    )(x, mod_kv, mod_kv, gn, wkv, gk2)


def _o_kernel(group, batch_major, a_ref, x_ref, gt_ref, w_ref, o_ref, *slabs):
    if batch_major:
        a = _load_step_major(a_ref, slabs[0], group).astype(bf16)
    else:
        a = a_ref[...]
    out = jnp.dot(a, w_ref[...], preferred_element_type=f32)
    o_ref[...] = x_ref[...] + _gate(out, gt_ref[0], group)


def _o_proj(attn, x, mod, mod_rows, layer, block, group, steps, wo, batch_major):
    rows_total, d = x.shape
    width = wo.shape[1]
    rows = group * steps
    tile = pl.BlockSpec((rows, d), lambda t: (t, 0))
    if batch_major:
        a_spec = pl.BlockSpec((group, steps, width), lambda t: (0, t, 0))
        scratch = [pltpu.VMEM((width // LANES, rows, LANES), f32)]
    else:
        a_spec = pl.BlockSpec((rows, width), lambda t: (t, 0))
        scratch = []
    return pl.pallas_call(
        functools.partial(_o_kernel, group, batch_major),
        out_shape=jax.ShapeDtypeStruct((rows_total, d), f32),
        grid=(rows_total // rows,),
        in_specs=[a_spec, tile, _mod_spec(group, mod_rows, layer, 2, d), _layer_spec(wo.shape, block)],
        out_specs=tile,
        scratch_shapes=scratch,
        compiler_params=_params("arbitrary"),
        name="o_proj",
    )(attn, x, mod, wo)


def _lambda(lp, lam_init):
    s1 = jnp.sum(lp[0:1] * lp[1:2], axis=-1, keepdims=True)
    s2 = jnp.sum(lp[2:3] * lp[3:4], axis=-1, keepdims=True)
    return jnp.exp(s1) - jnp.exp(s2) + lam_init


def _sub_norm(o, gain, lam_init):
    return o * lax.rsqrt(jnp.mean(o * o, axis=-1, keepdims=True) + EPS) * gain * (1.0 - lam_init)


def _prompt_attn_kernel(lam_init, tile, slopes_ref, lp_ref, gs_ref, q_ref, k_ref, v_ref, o_ref,
                        bias_scr, diag_scr):
    head = pl.program_id(0)
    seq = pl.program_id(1)
    slope2 = slopes_ref[head] * LOG2E
    n_tiles = q_ref.shape[0] // tile

    @pl.when(seq == 0)
    def _():
        dist = (lax.broadcasted_iota(jnp.int32, (tile, tile), 0)
                - lax.broadcasted_iota(jnp.int32, (tile, tile), 1))
        bias = -slope2 * dist.astype(f32)
        masked = jnp.where(dist >= 0, bias, MASKED)
        for c in range(2):
            bias_scr[c * tile:(c + 1) * tile, :] = bias
            diag_scr[c * tile:(c + 1) * tile, :] = masked

    lam = _lambda(lp_ref[...], lam_init)
    for qi in range(n_tiles):
        q = q_ref[qi * tile:(qi + 1) * tile, :]
        lane = lax.broadcasted_iota(jnp.int32, q.shape, 1)
        zero = jnp.zeros_like(q)
        q_maps = jnp.concatenate([jnp.where(lane < LANES // 2, q, zero),
                                  jnp.where(lane < LANES // 2, zero, q)], axis=0)
        m = jnp.full((2 * tile, 1), MASKED, f32)
        l = jnp.zeros((2 * tile, LANES), f32)
        acc = jnp.zeros((2 * tile, LANES), f32)
        for j in range(qi + 1):
            kt = k_ref[j * tile:(j + 1) * tile, :]
            vt = v_ref[j * tile:(j + 1) * tile, :]
            bias_ref = diag_scr if j == qi else bias_scr
            offset = -slope2 * float((qi - j) * tile)
            s = lax.dot_general(q_maps, kt, (((1,), (1,)), ((), ())), preferred_element_type=f32)
            s = s + bias_ref[...]
            m_new = jnp.maximum(m, jnp.max(s, axis=-1, keepdims=True) + offset)
            alpha = jnp.exp2(m - m_new)
            p = jnp.exp2(s - (m_new - offset))
            p_lanes = p[:, 0:LANES]
            for i in range(1, tile // LANES):
                p_lanes = p_lanes + p[:, i * LANES:(i + 1) * LANES]
            l = alpha * l + p_lanes
            acc = alpha * acc + jnp.dot(p.astype(bf16), vt, preferred_element_type=f32)
            m = m_new
        normed = acc / jnp.sum(l, axis=-1, keepdims=True)
        o = normed[:tile] - lam * normed[tile:]
        o_ref[qi * tile:(qi + 1) * tile, :] = _sub_norm(o, gs_ref[...], lam_init).astype(o_ref.dtype)


def _prompt_attention(q, k, v, slopes, lam_params, gsub, lam_init, n_heads, tile):
    n_seq, t_len, _ = q.shape
    spec = pl.BlockSpec((None, t_len, LANES), lambda h, g: (g, 0, h))
    return pl.pallas_call(
        functools.partial(_prompt_attn_kernel, lam_init, tile),
        out_shape=jax.ShapeDtypeStruct(q.shape, bf16),
        grid=(n_heads, n_seq),
        in_specs=[pl.BlockSpec(memory_space=pltpu.SMEM), _const_spec(lam_params.shape), _const_spec(gsub.shape),
                  spec, spec, spec],
        out_specs=spec,
        scratch_shapes=[pltpu.VMEM((2 * tile, tile), f32), pltpu.VMEM((2 * tile, tile), f32)],
        compiler_params=_params("arbitrary", "arbitrary"),
        name="prompt_attention",
    )(slopes, lam_params, gsub, q, k, v)


def _decode_attn_kernel(lam_init, n_pages, n_key_steps, past_len,
                        pt_ref, lp_ref, gs_ref, slope_ref, qt_ref, q_ref, kn_ref, vn_ref, *refs):
    k_refs = refs[:n_pages]
    v_refs = refs[n_pages:2 * n_pages]
    o_ref = refs[2 * n_pages]
    qcol_scr, s_scr, w_scr, wself_scr, acc_scr = refs[2 * n_pages + 1:]
    b = pl.program_id(0)
    g = pl.program_id(1)
    _, n_heads, n_maps, head_dim, _ = k_refs[0].shape

    @pl.when(g == 0)
    def _():
        lane = lax.broadcasted_iota(jnp.int32, qt_ref.shape, 1)
        col = jnp.sum(jnp.where(lane == b, qt_ref[...], 0.0), axis=-1, keepdims=True)
        qcol_scr[...] = jnp.broadcast_to(col, qcol_scr.shape)

    @pl.when(g < n_key_steps)
    def _():
        for i in range(n_pages):
            start = pl.multiple_of((g * n_pages + i) * PAGE_SIZE, PAGE_SIZE)
            for h in range(n_heads):
                for c in range(n_maps):
                    r = (h * n_maps + c) * head_dim
                    prod = k_refs[i][0, h, c] * qcol_scr[r:r + head_dim, :]
                    s_scr[pl.ds(c * n_heads + h, 1), pl.ds(start, PAGE_SIZE)] = (
                        jnp.sum(prod, axis=0, keepdims=True))

    @pl.when(g == n_key_steps - 1)
    def _():
        s = s_scr[...]
        pos = lax.broadcasted_iota(jnp.int32, s.shape, 1)
        s = s - slope_ref[...] * (past_len - pos).astype(f32)
        prod = q_ref[0] * kn_ref[0]
        low = lax.broadcasted_iota(jnp.int32, prod.shape, 1) < head_dim
        s_self = jnp.concatenate([jnp.sum(jnp.where(low, prod, 0.0), axis=-1, keepdims=True),
                                  jnp.sum(jnp.where(low, 0.0, prod), axis=-1, keepdims=True)], axis=0)
        m = jnp.maximum(jnp.max(s, axis=-1, keepdims=True), s_self)
        p = jnp.exp(s - m)
        p_self = jnp.exp(s_self - m)
        inv = 1.0 / (jnp.sum(p, axis=-1, keepdims=True) + p_self)
        lam = _lambda(lp_ref[...], lam_init)
        w_scr[...] = p[:n_heads] * inv[:n_heads] - lam * (p[n_heads:] * inv[n_heads:])
        w_self = p_self[:n_heads] * inv[:n_heads] - lam * (p_self[n_heads:] * inv[n_heads:])
        wself_scr[...] = jnp.broadcast_to(w_self, wself_scr.shape)
        acc_scr[...] = jnp.zeros_like(acc_scr)

    @pl.when(g >= n_key_steps)
    def _():
        lane = lax.broadcasted_iota(jnp.int32, (n_heads, PAGE_SIZE), 1)
        ones = jnp.ones((PAGE_SIZE, LANES), bf16)
        acc = jnp.zeros(acc_scr.shape, f32)
        for i in range(n_pages):
            start = pl.multiple_of(((g - n_key_steps) * n_pages + i) * PAGE_SIZE, PAGE_SIZE)
            w_page = w_scr[:, pl.ds(start, PAGE_SIZE)]
            one_token = jnp.concatenate([jnp.where(lane == t, w_page, 0.0) for t in range(PAGE_SIZE)], axis=0)
            spread = jnp.dot(one_token.astype(bf16), ones, preferred_element_type=f32)
            acc = acc + jnp.sum(spread.reshape(PAGE_SIZE, n_heads, LANES) * v_refs[i][...], axis=0)
        acc_scr[...] += acc

    @pl.when(g == pl.num_programs(1) - 1)
    def _():
        o = acc_scr[...] + wself_scr[...] * vn_ref[0]
        o_ref[0] = _sub_norm(o, gs_ref[...], lam_init).astype(o_ref.dtype)


def _decode_attention(q, k_new, v_new, cache_k, cache_v, page_table, slopes_np, lam_params, gsub, lam_init,
                      pages_per_step):
    n_b, width = q.shape
    n_pool, page, n_heads, n_maps, head_dim = cache_k.shape
    n_pg = page_table.shape[1]
    past_len = n_pg * page
    n_key_steps = n_pg // pages_per_step
    head_w = n_maps * head_dim
    k_pages = jnp.transpose(cache_k, (0, 2, 3, 4, 1))
    v_tokens = cache_v.reshape(n_pool * page, n_heads, head_w)
    heads = lambda a: a.reshape(n_b, n_heads, head_w)
    slope_rows = jnp.asarray(np.tile(slopes_np, n_maps)[:, None], f32)
    pt_flat = page_table.reshape(-1)
    consts = [lam_params, gsub, slope_rows, q.T]
    sample = pl.BlockSpec((1, n_heads, head_w), lambda b, g, pt: (b, 0, 0))

    def key_page(i):
        def index(b, g, pt):
            step = jnp.minimum(g, n_key_steps - 1)
            return (pt[b * n_pg + step * pages_per_step + i], 0, 0, 0, 0)
        return pl.BlockSpec((1, n_heads, n_maps, head_dim, page), index)

    def value_page(i):
        def index(b, g, pt):
            step = jnp.maximum(g - n_key_steps, 0)
            return (pt[b * n_pg + step * pages_per_step + i], 0, 0)
        return pl.BlockSpec((page, n_heads, head_w), index)

    grid_spec = pltpu.PrefetchScalarGridSpec(
        num_scalar_prefetch=1,
        grid=(n_b, 2 * n_key_steps),
        in_specs=[pl.BlockSpec(a.shape, lambda b, g, pt, n=a.ndim: (0,) * n) for a in consts]
                 + [sample, sample, sample]
                 + [key_page(i) for i in range(pages_per_step)]
                 + [value_page(i) for i in range(pages_per_step)],
        out_specs=sample,
        scratch_shapes=[pltpu.VMEM((width, LANES), f32),
                        pltpu.VMEM((n_maps * n_heads, past_len), f32),
                        pltpu.VMEM((n_heads, past_len), f32),
                        pltpu.VMEM((n_heads, head_w), f32),
                        pltpu.VMEM((n_heads, head_w), f32)],
    )
    return pl.pallas_call(
        functools.partial(_decode_attn_kernel, lam_init, pages_per_step, n_key_steps, past_len),
        out_shape=jax.ShapeDtypeStruct((n_b, n_heads, head_w), bf16),
        grid_spec=grid_spec,
        compiler_params=_params("arbitrary", "arbitrary"),
        name="decode_attention",
    )(pt_flat, *consts, heads(q), heads(k_new), heads(v_new),
      *([k_pages] * pages_per_step), *([v_tokens] * pages_per_step))


def _time_major(a):
    return jnp.swapaxes(a, 0, 1).reshape((a.shape[0] * a.shape[1],) + a.shape[2:])


def _batch_major(a, n_seq):
    return jnp.swapaxes(a.reshape((a.shape[0] // n_seq, n_seq) + a.shape[1:]), 0, 1)


def kernel(x_prompt, x_sample, c_prompt, c_sample, cache_k, cache_v, page_table, state_lru_h, state_lru_conv, state_ffn_conv, w_ada, b_ada, g_norm_mix, g_norm_ffn, w_lru_in, w_lru_conv, b_lru_conv, w_gate_x, b_gate_x, w_gate_a, b_gate_a, lru_log_param, w_lru_out, w_ada_kv, b_ada_kv, g_norm_kv, w_kv, g_k_norm, w_q, g_q_norm, lam_q1, lam_k1, lam_q2, lam_k2, g_subln, w_o, w_up, w_ffn_conv, b_ffn_conv, w_down):
    depth, d, _ = w_ada.shape
    n_a = w_lru_in.shape[0]
    n_bp, seq, _ = x_prompt.shape
    n_bs, dec_seq, _ = x_sample.shape
    assert dec_seq == 1 and n_bp == 8 and n_bs % 8 == 0 and n_bs <= LANES
    d_lru = w_lru_out.shape[1]
    d_ff = w_down.shape[1]
    head_dim = g_q_norm.shape[1]
    assert 2 * head_dim == LANES and cache_k.shape[1] == PAGE_SIZE
    attn_w = w_q.shape[2]
    n_heads = attn_w // LANES
    n_conv_lru = w_lru_conv.shape[1]
    n_conv_ffn = w_ffn_conv.shape[1]

    c_all = jnp.concatenate([c_sample, c_prompt, jnp.zeros((8, d), f32)], axis=0)
    mod = _ada(c_all, w_ada, b_ada.reshape(depth, 1, 6 * d), 1536)
    mod_kv = _ada(c_all, w_ada_kv[None], b_ada_kv.reshape(1, 1, 2 * d), 1024)
    prompt_rows = n_bs // n_bp

    win16 = w_lru_in.astype(bf16)
    wg16 = jnp.concatenate([w_gate_x, w_gate_a], axis=-1).astype(bf16)
    wout16 = w_lru_out.astype(bf16)
    wkv16 = w_kv.astype(bf16)
    wq16 = w_q.astype(bf16)
    wo16 = w_o.astype(bf16)
    wup16 = w_up.astype(bf16)
    wdown16 = w_down.astype(bf16)
    rows3 = lambda a: a[:, None, :]
    slopes_np = (2.0 ** (-8.0 * np.arange(1, n_heads + 1, dtype=np.float64) / n_heads)).astype(np.float32)
    slopes = jnp.asarray(slopes_np)
    head_scale = head_dim ** -0.5

    def run_group(x, group, mod_rows, steps_lru, steps_ffn, steps_proj, lru_h0, lru_conv0, ffn_conv0, attend,
                  batch_major):
        lru_h, lru_conv, ffn_conv = [], [], []
        kv = None
        for l in range(depth):
            if l < n_a:
                x, h_last, conv_new = _lru_layer(
                    x, mod, mod_rows, l, group, steps_lru, rows3(g_norm_mix), win16, w_lru_conv,
                    rows3(b_lru_conv), wg16, rows3(b_gate_x), rows3(b_gate_a),
                    rows3(lru_log_param), wout16, lru_h0[l], lru_conv0[l], batch_major and l == 0)
                lru_h.append(h_last)
                lru_conv.append(conv_new)
            else:
                j = l - n_a
                lam_init = 0.8 - 0.6 * math.exp(-0.3 * l)
                lam_params = jnp.stack([lam_q1[j], lam_k1[j], lam_q2[j], lam_k2[j]])
                gq2 = jnp.tile(g_q_norm[j], 2)[None]
                attn = attend(x, l, j, gq2, lam_params, g_subln[j][None], lam_init, kv)
                x = _o_proj(attn, x, mod, mod_rows, l, j, group, steps_proj, wo16, batch_major)
            x, fb = _ffn_layer(x, mod, mod_rows, l, group, steps_ffn, 512, rows3(g_norm_ffn), wup16,
                               w_ffn_conv, rows3(b_ffn_conv), wdown16, ffn_conv0[l],
                               batch_major and l == depth - 1)
            ffn_conv.append(fb)
            if l == n_a - 1:
                kv = _kv_proj(x, mod_kv, mod_rows, group, steps_proj, g_norm_kv[None], wkv16,
                              jnp.tile(g_k_norm, 2)[None], batch_major)
        return x, kv[0], kv[1], lru_h, lru_conv, ffn_conv

    steps_proj_p = min(64, seq)

    def attend_prompt(x, l, j, gq2, lam_params, gsub, lam_init, kv):
        q = _q_proj(x, mod, prompt_rows, l, j, n_bp, steps_proj_p, rows3(g_norm_mix), wq16, gq2,
                    head_scale * LOG2E, bf16, True)
        return _prompt_attention(q, kv[2], kv[3], slopes, lam_params, gsub, lam_init, n_heads, min(512, seq))

    zeros_p = lambda n, r, w: [jnp.zeros((r, w), f32)] * n
    y_prompt, kp, vp, hp, lcp, fcp = run_group(
        x_prompt, n_bp, prompt_rows, min(64, seq), min(128, seq), steps_proj_p,
        zeros_p(n_a, n_bp, d_lru), zeros_p(n_a, (n_conv_lru - 1) * n_bp, d_lru),
        zeros_p(depth, (n_conv_ffn - 1) * n_bp, 2 * d_ff), attend_prompt, True)
    k_prompt = kp.reshape(n_bp, seq, n_heads, 2, head_dim)
    v_prompt = vp.reshape(n_bp, seq, n_heads, 2 * head_dim)
    state_bm = lambda s: jnp.stack([_batch_major(a, n_bp) for a in s])
    h_p = jnp.stack(hp)
    lc_p = state_bm(lcp)
    fc_p = state_bm(fcp)

    def attend_sample(x, l, j, gq2, lam_params, gsub, lam_init, kv):
        q = _q_proj(x, mod, 0, l, j, n_bs, 1, rows3(g_norm_mix), wq16, gq2, head_scale, f32, False)
        o = _decode_attention(q, kv[0], kv[1], cache_k, cache_v, page_table, slopes_np,
                              lam_params, gsub, lam_init, min(16, page_table.shape[1]))
        return o.reshape(n_bs, attn_w)

    xs = x_sample.reshape(n_bs, d)
    ys, ks, vs, hs, lcs, fcs = run_group(
        xs, n_bs, 0, 1, 1, 1,
        [state_lru_h[l] for l in range(n_a)],
        [_time_major(state_lru_conv[l]) for l in range(n_a)],
        [_time_major(state_ffn_conv[l]) for l in range(depth)], attend_sample, False)
    y_sample = ys.reshape(n_bs, 1, d)
    k_sample = ks.reshape(n_bs, 1, n_heads, 2, head_dim)
    v_sample = vs.reshape(n_bs, 1, n_heads, 2 * head_dim)
    state_bs = lambda s: jnp.stack([_batch_major(a, n_bs) for a in s])
    h_s = jnp.stack(hs)
    lc_s = state_bs(lcs)
    fc_s = state_bs(fcs)

    return (y_prompt, y_sample, k_prompt, v_prompt, k_sample, v_sample,
            h_p, h_s, lc_p, lc_s, fc_p, fc_s)
```

```python
import functools
import math

import numpy as np
import jax
import jax.numpy as jnp
from jax import lax
from jax.experimental import pallas as pl
from jax.experimental.pallas import tpu as pltpu

f32 = jnp.float32
bf16 = jnp.bfloat16

EPS = 1e-6
LRU_C = 8.0
LANES = 128
PAGE_SIZE = 128
MASKED = -1e30
LOG2E = math.log2(math.e)
VMEM_LIMIT_BYTES = 56 * 1024 * 1024


def _params(*semantics):
    return pltpu.CompilerParams(dimension_semantics=semantics,
                                vmem_limit_bytes=VMEM_LIMIT_BYTES)


def _const_spec(shape):
    zeros = (0,) * len(shape)
    return pl.BlockSpec(shape, lambda *_: zeros)


def _layer_spec(stacked_shape, layer):
    index = (layer,) + (0,) * (len(stacked_shape) - 1)
    return pl.BlockSpec((None,) + tuple(stacked_shape[1:]), lambda *_: index)


def _sigmoid(x):
    return 1.0 / (1.0 + jnp.exp(-x))


def _gelu_tanh(x):
    return 0.5 * x * (1.0 + jnp.tanh(math.sqrt(2.0 / math.pi) * (x + 0.044715 * (x * x * x))))


def _mod_norm(x, gain, shift, scale, group):
    rows, d = x.shape
    y = x * lax.rsqrt(jnp.mean(x * x, axis=-1, keepdims=True) + EPS) * gain
    y = y.reshape(rows // group, group, d) * (1.0 + scale)[None] + shift[None]
    return y.reshape(rows, d)


def _gate(x, gate, group):
    rows, d = x.shape
    return (x.reshape(rows // group, group, d) * gate[None]).reshape(rows, d)


def _head_half_norm(y, gain2):
    lane = lax.broadcasted_iota(jnp.int32, y.shape, 1)
    low = lane < (LANES // 2)
    sq = y * y
    s_low = jnp.sum(jnp.where(low, sq, 0.0), axis=-1, keepdims=True)
    s_high = jnp.sum(jnp.where(low, 0.0, sq), axis=-1, keepdims=True)
    inv = lax.rsqrt(jnp.where(low, s_low, s_high) * (2.0 / LANES) + EPS)
    return y * inv * gain2


def _store_batch_major(o_ref, slabs_ref, group):
    steps = slabs_ref.shape[1] // group
    for g in range(group):
        for j in range(slabs_ref.shape[0]):
            o_ref[g, :, j * LANES:(j + 1) * LANES] = (
                slabs_ref.at[j][pl.ds(g, steps, stride=group), :].astype(o_ref.dtype))


def _load_step_major(a_ref, slabs_ref, group):
    steps = a_ref.shape[1]
    n_slabs = slabs_ref.shape[0]
    for g in range(group):
        for j in range(n_slabs):
            slabs_ref.at[j][pl.ds(g, steps, stride=group), :] = (
                a_ref[g, :, j * LANES:(j + 1) * LANES].astype(f32))
    return jnp.concatenate([slabs_ref[j] for j in range(n_slabs)], axis=-1)


def _ada_kernel(c_ref, w_ref, b_ref, o_ref):
    c = c_ref[...]
    sc = (c * _sigmoid(c)).astype(bf16)
    o_ref[0] = jnp.dot(sc, w_ref[0].astype(bf16), preferred_element_type=f32) + b_ref[0]


def _ada(c, w, b, tn):
    n_layers, d, n = w.shape
    m = c.shape[0]
    return pl.pallas_call(
        _ada_kernel,
        out_shape=jax.ShapeDtypeStruct((n_layers, m, n), f32),
        grid=(n_layers, n // tn),
        in_specs=[_const_spec((m, d)),
                  pl.BlockSpec((1, d, tn), lambda l, j: (l, 0, j)),
                  pl.BlockSpec((1, 1, tn), lambda l, j: (l, 0, j))],
        out_specs=pl.BlockSpec((1, m, tn), lambda l, j: (l, 0, j)),
        compiler_params=_params("arbitrary", "arbitrary"),
        name="ada_mod",
    )(c, w, b)


def _mod_spec(group, row_block, layer, piece, d):
    return pl.BlockSpec((1, group, d), lambda *_: (layer, row_block, piece))


def _lru_kernel(group, steps, n_conv, batch_major_in,
                x_ref, sh_ref, sc_ref, gt_ref, gn_ref, win_ref, cw_ref, cb_ref,
                wg_ref, bgx_ref, bga_ref, lp_ref, wout_ref, h0_ref, c0_ref,
                o_ref, hlast_ref, cnew_ref,
                xw_scr, halo_scr, a_scr, b_scr, gb_scr, y_scr, h_scr, *slabs):
    rows = group * steps
    d_lru = a_scr.shape[1]
    halo = (n_conv - 1) * group
    t = pl.program_id(0)

    @pl.when(t == 0)
    def _():
        h_scr[...] = h0_ref[...]
        halo_scr[...] = c0_ref[...]

    x = _load_step_major(x_ref, slabs[0], group) if batch_major_in else x_ref[...]
    h = _mod_norm(x, gn_ref[...], sh_ref[0], sc_ref[0], group).astype(bf16)
    u = jnp.dot(h, win_ref[...], preferred_element_type=f32)
    gb_scr[...] = u[:, d_lru:]

    xw_scr[0:halo, :] = halo_scr[...]
    xw_scr[halo:halo + rows, :] = u[:, :d_lru]
    xc = cb_ref[...] + cw_ref[0:1, :] * xw_scr[0:rows, :]
    for j in range(1, n_conv):
        xc = xc + cw_ref[j:j + 1, :] * xw_scr[j * group:j * group + rows, :]
    new_halo = xw_scr[rows:rows + halo, :]
    halo_scr[...] = new_halo
    cnew_ref[...] = new_halo

    lp = lp_ref[...]
    log_sig = jnp.minimum(lp, 0.0) - jnp.log1p(jnp.exp(-jnp.abs(lp)))
    n_blocks = wg_ref.shape[0]
    blk = d_lru // n_blocks
    for n in range(n_blocks):
        cols = slice(n * blk, (n + 1) * blk)
        xn = xc[:, cols]
        gates = jnp.dot(xn.astype(bf16), wg_ref[n], preferred_element_type=f32)
        gate_x = _sigmoid(gates[:, :blk] + bgx_ref[:, cols])
        gate_a = _sigmoid(gates[:, blk:] + bga_ref[:, cols])
        log_a = LRU_C * gate_a * log_sig[:, cols]
        a = jnp.exp(log_a)
        a_scr[:, cols] = a
        b_scr[:, cols] = jnp.sqrt(-jnp.tanh(log_a) * (a * a + 1.0)) * (gate_x * xn)

    def step(i, hprev):
        r = pl.multiple_of(i * group, group)
        hnew = a_scr[pl.ds(r, group), :] * hprev + b_scr[pl.ds(r, group), :]
        b_scr[pl.ds(r, group), :] = hnew
        return hnew

    if steps == 1:
        hlast = step(0, h_scr[...])
    else:
        hlast = lax.fori_loop(0, steps, step, h_scr[...], unroll=8)
    h_scr[...] = hlast
    hlast_ref[...] = hlast

    y_scr[...] = (b_scr[...] * _gelu_tanh(gb_scr[...])).astype(bf16)
    out = jnp.dot(y_scr[...], wout_ref[...], preferred_element_type=f32)
    o_ref[...] = x + _gate(out, gt_ref[0], group)


def _lru_layer(x, mod, mod_rows, layer, group, steps, gn, win, cw, cb, wg, bgx, bga, lp, wout, h0, c0,
               batch_major_in):
    d = x.shape[-1]
    rows_total = x.shape[0] * x.shape[1] if batch_major_in else x.shape[0]
    d_lru = wout.shape[1]
    n_conv = cw.shape[1]
    rows = group * steps
    halo = (n_conv - 1) * group
    spec = lambda p: _mod_spec(group, mod_rows, layer, p, d)
    par = lambda a: _layer_spec(a.shape, layer)
    tile = pl.BlockSpec((rows, d), lambda t: (t, 0))
    if batch_major_in:
        x_spec = pl.BlockSpec((group, steps, d), lambda t: (0, t, 0))
        slabs = [pltpu.VMEM((d // LANES, rows, LANES), f32)]
    else:
        x_spec, slabs = tile, []
    kernel = functools.partial(_lru_kernel, group, steps, n_conv, batch_major_in)
    return pl.pallas_call(
        kernel,
        out_shape=(jax.ShapeDtypeStruct((rows_total, d), f32),
                   jax.ShapeDtypeStruct((group, d_lru), f32),
                   jax.ShapeDtypeStruct((halo, d_lru), f32)),
        grid=(rows_total // rows,),
        in_specs=[x_spec, spec(0), spec(1), spec(2), par(gn), par(win), par(cw), par(cb), par(wg), par(bgx),
                  par(bga), par(lp), par(wout), _const_spec(h0.shape), _const_spec(c0.shape)],
        out_specs=(tile, _const_spec((group, d_lru)), _const_spec((halo, d_lru))),
        scratch_shapes=[pltpu.VMEM((rows + halo, d_lru), f32), pltpu.VMEM((halo, d_lru), f32),
                        pltpu.VMEM((rows, d_lru), f32), pltpu.VMEM((rows, d_lru), f32),
                        pltpu.VMEM((rows, d_lru), f32), pltpu.VMEM((rows, d_lru), bf16),
                        pltpu.VMEM((group, d_lru), f32)] + slabs,
        compiler_params=_params("arbitrary"),
        name="rglru_layer",
    )(x, mod, mod, mod, gn, win, cw, cb, wg, bgx, bga, lp, wout, h0, c0)


def _ffn_kernel(group, steps, n_conv, fc, batch_major_out,
                x_ref, sh_ref, sc_ref, gt_ref, gn_ref, wup_ref, cw_ref, cb_ref, wd_ref, c0_ref,
                o_ref, cn_ref,
                uwa_scr, uwg_scr, halo_scr, *slabs):
    rows = group * steps
    halo = (n_conv - 1) * group
    d_ff = wd_ref.shape[0]

    @pl.when(pl.program_id(0) == 0)
    def _():
        halo_scr[...] = c0_ref[...]

    x = x_ref[...]
    h = _mod_norm(x, gn_ref[...], sh_ref[0], sc_ref[0], group).astype(bf16)

    def conv_half(col0, uw_scr):
        cols = slice(col0, col0 + fc)
        uw_scr[0:halo, :] = halo_scr[:, cols]
        uw_scr[halo:halo + rows, :] = jnp.dot(h, wup_ref[:, cols], preferred_element_type=f32)
        y = cb_ref[:, cols] + cw_ref[0:1, cols] * uw_scr[0:rows, :]
        for j in range(1, n_conv):
            y = y + cw_ref[j:j + 1, cols] * uw_scr[j * group:j * group + rows, :]
        new_halo = uw_scr[rows:rows + halo, :]
        halo_scr[:, cols] = new_halo
        cn_ref[:, cols] = new_halo
        return y

    acc = jnp.zeros(x.shape, f32)
    for c in range(d_ff // fc):
        a = conv_half(c * fc, uwa_scr)
        g = conv_half(d_ff + c * fc, uwg_scr)
        act = (_gelu_tanh(a) * g).astype(bf16)
        acc = acc + jnp.dot(act, wd_ref[c * fc:(c + 1) * fc, :], preferred_element_type=f32)

    y = x + _gate(acc, gt_ref[0], group)
    if batch_major_out:
        for j in range(slabs[0].shape[0]):
            slabs[0][j] = y[:, j * LANES:(j + 1) * LANES]
        _store_batch_major(o_ref, slabs[0], group)
    else:
        o_ref[...] = y


def _ffn_layer(x, mod, mod_rows, layer, group, steps, fc, gn, wup, cw, cb, wdown, c0, batch_major_out):
    rows_total, d = x.shape
    d_ff = wdown.shape[1]
    n_conv = cw.shape[1]
    rows = group * steps
    halo = (n_conv - 1) * group
    spec = lambda p: _mod_spec(group, mod_rows, layer, p, d)
    tile = pl.BlockSpec((rows, d), lambda t: (t, 0))
    resident = lambda a: pl.BlockSpec((None,) + tuple(a.shape[1:]), lambda t: (layer,) + (0,) * (a.ndim - 1),
                                      pipeline_mode=pl.Buffered(1))
    state = pl.BlockSpec((halo, 2 * d_ff), lambda t: (0, 0))
    if batch_major_out:
        out_shape = jax.ShapeDtypeStruct((group, rows_total // group, d), f32)
        out_spec = pl.BlockSpec((group, steps, d), lambda t: (0, t, 0))
        slabs = [pltpu.VMEM((d // LANES, rows, LANES), f32)]
    else:
        out_shape, out_spec, slabs = jax.ShapeDtypeStruct((rows_total, d), f32), tile, []
    kernel = functools.partial(_ffn_kernel, group, steps, n_conv, fc, batch_major_out)
    return pl.pallas_call(
        kernel,
        out_shape=(out_shape, jax.ShapeDtypeStruct((halo, 2 * d_ff), f32)),
        grid=(rows_total // rows,),
        in_specs=[tile, spec(3), spec(4), spec(5), _layer_spec(gn.shape, layer),
                  resident(wup), resident(cw), resident(cb), resident(wdown), state],
        out_specs=(out_spec, state),
        scratch_shapes=[pltpu.VMEM((rows + halo, fc), f32), pltpu.VMEM((rows + halo, fc), f32),
                        pltpu.VMEM((halo, 2 * d_ff), f32)] + slabs,
        compiler_params=_params("arbitrary"),
        name="conv_ffn",
    )(x, mod, mod, mod, gn, wup, cw, cb, wdown, c0)


def _q_kernel(group, scale, batch_major, x_ref, sh_ref, sc_ref, gn_ref, w_ref, gq_ref, o_ref, *slabs):
    h = _mod_norm(x_ref[...], gn_ref[...], sh_ref[0], sc_ref[0], group).astype(bf16)
    q = jnp.dot(h, w_ref[...], preferred_element_type=f32)
    for j in range(q.shape[1] // LANES):
        cols = slice(j * LANES, (j + 1) * LANES)
        qj = _head_half_norm(q[:, cols], gq_ref[...]) * scale
        if batch_major:
            slabs[0][j] = qj
        else:
            o_ref[:, cols] = qj.astype(o_ref.dtype)
    if batch_major:
        _store_batch_major(o_ref, slabs[0], group)


def _q_proj(x, mod, mod_rows, layer, block, group, steps, gn, wq, gq2, scale, out_dtype, batch_major):
    rows_total, d = x.shape
    width = wq.shape[2]
    rows = group * steps
    spec = lambda p: _mod_spec(group, mod_rows, layer, p, d)
    if batch_major:
        out_shape = jax.ShapeDtypeStruct((group, rows_total // group, width), out_dtype)
        out_spec = pl.BlockSpec((group, steps, width), lambda t: (0, t, 0))
        scratch = [pltpu.VMEM((width // LANES, rows, LANES), f32)]
    else:
        out_shape = jax.ShapeDtypeStruct((rows_total, width), out_dtype)
        out_spec = pl.BlockSpec((rows, width), lambda t: (t, 0))
        scratch = []
    return pl.pallas_call(
        functools.partial(_q_kernel, group, scale, batch_major),
        out_shape=out_shape,
        grid=(rows_total // rows,),
        in_specs=[pl.BlockSpec((rows, d), lambda t: (t, 0)), spec(0), spec(1), _layer_spec(gn.shape, layer),
                  _layer_spec(wq.shape, block), _const_spec(gq2.shape)],
        out_specs=out_spec,
        scratch_shapes=scratch,
        compiler_params=_params("arbitrary"),
        name="q_proj",
    )(x, mod, mod, gn, wq, gq2)


def _kv_kernel(group, batch_major, x_ref, sh_ref, sc_ref, gn_ref, w_ref, gk_ref,
               k_ref, v_ref, k16_ref, v16_ref, *slabs):
    h = _mod_norm(x_ref[...], gn_ref[...], sh_ref[0], sc_ref[0], group).astype(bf16)
    kv = jnp.dot(h, w_ref[...], preferred_element_type=f32)
    width = kv.shape[1] // 2
    for j in range(width // LANES):
        cols = slice(j * LANES, (j + 1) * LANES)
        kj = _head_half_norm(kv[:, cols], gk_ref[...])
        vj = kv[:, width + j * LANES:width + (j + 1) * LANES]
        if batch_major:
            slabs[0][j] = kj
            slabs[1][j] = vj
        else:
            k_ref[:, cols] = kj
            k16_ref[:, cols] = kj.astype(bf16)
            v_ref[:, cols] = vj
            v16_ref[:, cols] = vj.astype(bf16)
    if batch_major:
        _store_batch_major(k_ref, slabs[0], group)
        _store_batch_major(k16_ref, slabs[0], group)
        _store_batch_major(v_ref, slabs[1], group)
        _store_batch_major(v16_ref, slabs[1], group)


def _kv_proj(x, mod_kv, mod_rows, group, steps, gn, wkv, gk2, batch_major):
    rows_total, d = x.shape
    width = wkv.shape[1] // 2
    rows = group * steps
    spec = lambda p: _mod_spec(group, mod_rows, 0, p, d)
    if batch_major:
        shape = (group, rows_total // group, width)
        out_tile = pl.BlockSpec((group, steps, width), lambda t: (0, t, 0))
        scratch = [pltpu.VMEM((width // LANES, rows, LANES), f32)] * 2
    else:
        shape = (rows_total, width)
        out_tile = pl.BlockSpec((rows, width), lambda t: (t, 0))
        scratch = []
    return pl.pallas_call(
        functools.partial(_kv_kernel, group, batch_major),
        out_shape=(jax.ShapeDtypeStruct(shape, f32), jax.ShapeDtypeStruct(shape, f32),
                   jax.ShapeDtypeStruct(shape, bf16), jax.ShapeDtypeStruct(shape, bf16)),
        grid=(rows_total // rows,),
        in_specs=[pl.BlockSpec((rows, d), lambda t: (t, 0)), spec(0), spec(1), _const_spec(gn.shape),
                  _const_spec(wkv.shape), _const_spec(gk2.shape)],
        out_specs=(out_tile, out_tile, out_tile, out_tile),
        scratch_shapes=scratch,
        compiler_params=_params("arbitrary"),
        name="kv_proj",
    )(x, mod_kv, mod_kv, gn, wkv, gk2)


def _o_kernel(group, batch_major, a_ref, x_ref, gt_ref, w_ref, o_ref, *slabs):
    if batch_major:
        a = _load_step_major(a_ref, slabs[0], group).astype(bf16)
    else:
        a = a_ref[...]
    out = jnp.dot(a, w_ref[...], preferred_element_type=f32)
    o_ref[...] = x_ref[...] + _gate(out, gt_ref[0], group)


def _o_proj(attn, x, mod, mod_rows, layer, block, group, steps, wo, batch_major):
    rows_total, d = x.shape
    width = wo.shape[1]
    rows = group * steps
    tile = pl.BlockSpec((rows, d), lambda t: (t, 0))
    if batch_major:
        a_spec = pl.BlockSpec((group, steps, width), lambda t: (0, t, 0))
        scratch = [pltpu.VMEM((width // LANES, rows, LANES), f32)]
    else:
        a_spec = pl.BlockSpec((rows, width), lambda t: (t, 0))
        scratch = []
    return pl.pallas_call(
        functools.partial(_o_kernel, group, batch_major),
        out_shape=jax.ShapeDtypeStruct((rows_total, d), f32),
        grid=(rows_total // rows,),
        in_specs=[a_spec, tile, _mod_spec(group, mod_rows, layer, 2, d), _layer_spec(wo.shape, block)],
        out_specs=tile,
        scratch_shapes=scratch,
        compiler_params=_params("arbitrary"),
        name="o_proj",
    )(attn, x, mod, wo)


def _lambda(lp, lam_init):
    s1 = jnp.sum(lp[0:1] * lp[1:2], axis=-1, keepdims=True)
    s2 = jnp.sum(lp[2:3] * lp[3:4], axis=-1, keepdims=True)
    return jnp.exp(s1) - jnp.exp(s2) + lam_init


def _sub_norm(o, gain, lam_init):
    return o * lax.rsqrt(jnp.mean(o * o, axis=-1, keepdims=True) + EPS) * gain * (1.0 - lam_init)


def _prompt_attn_kernel(lam_init, tile, slopes_ref, lp_ref, gs_ref, q_ref, k_ref, v_ref, o_ref,
                        bias_scr, diag_scr):
    head = pl.program_id(0)
    seq = pl.program_id(1)
    slope2 = slopes_ref[head] * LOG2E
    n_tiles = q_ref.shape[0] // tile

    @pl.when(seq == 0)
    def _():
        dist = (lax.broadcasted_iota(jnp.int32, (tile, tile), 0)
                - lax.broadcasted_iota(jnp.int32, (tile, tile), 1))
        bias = -slope2 * dist.astype(f32)
        masked = jnp.where(dist >= 0, bias, MASKED)
        for c in range(2):
            bias_scr[c * tile:(c + 1) * tile, :] = bias
            diag_scr[c * tile:(c + 1) * tile, :] = masked

    lam = _lambda(lp_ref[...], lam_init)
    for qi in range(n_tiles):
        q = q_ref[qi * tile:(qi + 1) * tile, :]
        lane = lax.broadcasted_iota(jnp.int32, q.shape, 1)
        zero = jnp.zeros_like(q)
        q_maps = jnp.concatenate([jnp.where(lane < LANES // 2, q, zero),
                                  jnp.where(lane < LANES // 2, zero, q)], axis=0)
        m = jnp.full((2 * tile, 1), MASKED, f32)
        l = jnp.zeros((2 * tile, LANES), f32)
        acc = jnp.zeros((2 * tile, LANES), f32)
        for j in range(qi + 1):
            kt = k_ref[j * tile:(j + 1) * tile, :]
            vt = v_ref[j * tile:(j + 1) * tile, :]
            bias_ref = diag_scr if j == qi else bias_scr
            offset = -slope2 * float((qi - j) * tile)
            s = lax.dot_general(q_maps, kt, (((1,), (1,)), ((), ())), preferred_element_type=f32)
            s = s + bias_ref[...]
            m_new = jnp.maximum(m, jnp.max(s, axis=-1, keepdims=True) + offset)
            alpha = jnp.exp2(m - m_new)
            p = jnp.exp2(s - (m_new - offset))
            p_lanes = p[:, 0:LANES]
            for i in range(1, tile // LANES):
                p_lanes = p_lanes + p[:, i * LANES:(i + 1) * LANES]
            l = alpha * l + p_lanes
            acc = alpha * acc + jnp.dot(p.astype(bf16), vt, preferred_element_type=f32)
            m = m_new
        normed = acc / jnp.sum(l, axis=-1, keepdims=True)
        o = normed[:tile] - lam * normed[tile:]
        o_ref[qi * tile:(qi + 1) * tile, :] = _sub_norm(o, gs_ref[...], lam_init).astype(o_ref.dtype)


def _prompt_attention(q, k, v, slopes, lam_params, gsub, lam_init, n_heads, tile):
    n_seq, t_len, _ = q.shape
    spec = pl.BlockSpec((None, t_len, LANES), lambda h, g: (g, 0, h))
    return pl.pallas_call(
        functools.partial(_prompt_attn_kernel, lam_init, tile),
        out_shape=jax.ShapeDtypeStruct(q.shape, bf16),
        grid=(n_heads, n_seq),
        in_specs=[pl.BlockSpec(memory_space=pltpu.SMEM), _const_spec(lam_params.shape), _const_spec(gsub.shape),
                  spec, spec, spec],
        out_specs=spec,
        scratch_shapes=[pltpu.VMEM((2 * tile, tile), f32), pltpu.VMEM((2 * tile, tile), f32)],
        compiler_params=_params("arbitrary", "arbitrary"),
        name="prompt_attention",
    )(slopes, lam_params, gsub, q, k, v)


def _decode_attn_kernel(lam_init, n_pages, n_key_steps, past_len,
                        pt_ref, lp_ref, gs_ref, slope_ref, qt_ref, q_ref, kn_ref, vn_ref, *refs):
    k_refs = refs[:n_pages]
    v_refs = refs[n_pages:2 * n_pages]
    o_ref = refs[2 * n_pages]
    qcol_scr, s_scr, w_scr, wself_scr, acc_scr = refs[2 * n_pages + 1:]
    b = pl.program_id(0)
    g = pl.program_id(1)
    _, n_heads, n_maps, head_dim, _ = k_refs[0].shape

    @pl.when(g == 0)
    def _():
        lane = lax.broadcasted_iota(jnp.int32, qt_ref.shape, 1)
        col = jnp.sum(jnp.where(lane == b, qt_ref[...], 0.0), axis=-1, keepdims=True)
        qcol_scr[...] = jnp.broadcast_to(col, qcol_scr.shape)

    @pl.when(g < n_key_steps)
    def _():
        for i in range(n_pages):
            start = pl.multiple_of((g * n_pages + i) * PAGE_SIZE, PAGE_SIZE)
            for h in range(n_heads):
                for c in range(n_maps):
                    r = (h * n_maps + c) * head_dim
                    prod = k_refs[i][0, h, c] * qcol_scr[r:r + head_dim, :]
                    s_scr[pl.ds(c * n_heads + h, 1), pl.ds(start, PAGE_SIZE)] = (
                        jnp.sum(prod, axis=0, keepdims=True))

    @pl.when(g == n_key_steps - 1)
    def _():
        s = s_scr[...]
        pos = lax.broadcasted_iota(jnp.int32, s.shape, 1)
        s = s - slope_ref[...] * (past_len - pos).astype(f32)
        prod = q_ref[0] * kn_ref[0]
        low = lax.broadcasted_iota(jnp.int32, prod.shape, 1) < head_dim
        s_self = jnp.concatenate([jnp.sum(jnp.where(low, prod, 0.0), axis=-1, keepdims=True),
                                  jnp.sum(jnp.where(low, 0.0, prod), axis=-1, keepdims=True)], axis=0)
        m = jnp.maximum(jnp.max(s, axis=-1, keepdims=True), s_self)
        p = jnp.exp(s - m)
        p_self = jnp.exp(s_self - m)
        inv = 1.0 / (jnp.sum(p, axis=-1, keepdims=True) + p_self)
        lam = _lambda(lp_ref[...], lam_init)
        w_scr[...] = p[:n_heads] * inv[:n_heads] - lam * (p[n_heads:] * inv[n_heads:])
        w_self = p_self[:n_heads] * inv[:n_heads] - lam * (p_self[n_heads:] * inv[n_heads:])
        wself_scr[...] = jnp.broadcast_to(w_self, wself_scr.shape)
        acc_scr[...] = jnp.zeros_like(acc_scr)

    @pl.when(g >= n_key_steps)
    def _():
        lane = lax.broadcasted_iota(jnp.int32, (n_heads, PAGE_SIZE), 1)
        ones = jnp.ones((PAGE_SIZE, LANES), bf16)
        acc = jnp.zeros(acc_scr.shape, f32)
        for i in range(n_pages):
            start = pl.multiple_of(((g - n_key_steps) * n_pages + i) * PAGE_SIZE, PAGE_SIZE)
            w_page = w_scr[:, pl.ds(start, PAGE_SIZE)]
            one_token = jnp.concatenate([jnp.where(lane == t, w_page, 0.0) for t in range(PAGE_SIZE)], axis=0)
            spread = jnp.dot(one_token.astype(bf16), ones, preferred_element_type=f32)
            acc = acc + jnp.sum(spread.reshape(PAGE_SIZE, n_heads, LANES) * v_refs[i][...], axis=0)
        acc_scr[...] += acc

    @pl.when(g == pl.num_programs(1) - 1)
    def _():
        o = acc_scr[...] + wself_scr[...] * vn_ref[0]
        o_ref[0] = _sub_norm(o, gs_ref[...], lam_init).astype(o_ref.dtype)


def _decode_attention(q, k_new, v_new, cache_k, cache_v, page_table, slopes_np, lam_params, gsub, lam_init,
                      pages_per_step):
    n_b, width = q.shape
    n_pool, page, n_heads, n_maps, head_dim = cache_k.shape
    n_pg = page_table.shape[1]
    past_len = n_pg * page
    n_key_steps = n_pg // pages_per_step
    head_w = n_maps * head_dim
    k_pages = jnp.transpose(cache_k, (0, 2, 3, 4, 1))
    v_tokens = cache_v.reshape(n_pool * page, n_heads, head_w)
    heads = lambda a: a.reshape(n_b, n_heads, head_w)
    slope_rows = jnp.asarray(np.tile(slopes_np, n_maps)[:, None], f32)
    pt_flat = page_table.reshape(-1)
    consts = [lam_params, gsub, slope_rows, q.T]
    sample = pl.BlockSpec((1, n_heads, head_w), lambda b, g, pt: (b, 0, 0))

    def key_page(i):
        def index(b, g, pt):
            step = jnp.minimum(g, n_key_steps - 1)
            return (pt[b * n_pg + step * pages_per_step + i], 0, 0, 0, 0)
        return pl.BlockSpec((1, n_heads, n_maps, head_dim, page), index)

    def value_page(i):
        def index(b, g, pt):
            in_values = g >= n_key_steps
            step = jnp.where(in_values, g - n_key_steps, n_key_steps - 1)
            sample = jnp.where(in_values, b, jnp.maximum(b - 1, 0))
            return (pt[sample * n_pg + step * pages_per_step + i], 0, 0)
        return pl.BlockSpec((page, n_heads, head_w), index)

    grid_spec = pltpu.PrefetchScalarGridSpec(
        num_scalar_prefetch=1,
        grid=(n_b, 2 * n_key_steps),
        in_specs=[pl.BlockSpec(a.shape, lambda b, g, pt, n=a.ndim: (0,) * n) for a in consts]
                 + [sample, sample, sample]
                 + [key_page(i) for i in range(pages_per_step)]
                 + [value_page(i) for i in range(pages_per_step)],
        out_specs=sample,
        scratch_shapes=[pltpu.VMEM((width, LANES), f32),
                        pltpu.VMEM((n_maps * n_heads, past_len), f32),
                        pltpu.VMEM((n_heads, past_len), f32),
                        pltpu.VMEM((n_heads, head_w), f32),
                        pltpu.VMEM((n_heads, head_w), f32)],
    )
    return pl.pallas_call(
        functools.partial(_decode_attn_kernel, lam_init, pages_per_step, n_key_steps, past_len),
        out_shape=jax.ShapeDtypeStruct((n_b, n_heads, head_w), bf16),
        grid_spec=grid_spec,
        compiler_params=_params("arbitrary", "arbitrary"),
        name="decode_attention",
    )(pt_flat, *consts, heads(q), heads(k_new), heads(v_new),
      *([k_pages] * pages_per_step), *([v_tokens] * pages_per_step))


def _time_major(a):
    return jnp.swapaxes(a, 0, 1).reshape((a.shape[0] * a.shape[1],) + a.shape[2:])


def _batch_major(a, n_seq):
    return jnp.swapaxes(a.reshape((a.shape[0] // n_seq, n_seq) + a.shape[1:]), 0, 1)


def kernel(x_prompt, x_sample, c_prompt, c_sample, cache_k, cache_v, page_table, state_lru_h, state_lru_conv, state_ffn_conv, w_ada, b_ada, g_norm_mix, g_norm_ffn, w_lru_in, w_lru_conv, b_lru_conv, w_gate_x, b_gate_x, w_gate_a, b_gate_a, lru_log_param, w_lru_out, w_ada_kv, b_ada_kv, g_norm_kv, w_kv, g_k_norm, w_q, g_q_norm, lam_q1, lam_k1, lam_q2, lam_k2, g_subln, w_o, w_up, w_ffn_conv, b_ffn_conv, w_down):
    depth, d, _ = w_ada.shape
    n_a = w_lru_in.shape[0]
    n_bp, seq, _ = x_prompt.shape
    n_bs, dec_seq, _ = x_sample.shape
    assert dec_seq == 1 and n_bp == 8 and n_bs % 8 == 0 and n_bs <= LANES
    d_lru = w_lru_out.shape[1]
    d_ff = w_down.shape[1]
    head_dim = g_q_norm.shape[1]
    assert 2 * head_dim == LANES and cache_k.shape[1] == PAGE_SIZE
    attn_w = w_q.shape[2]
    n_heads = attn_w // LANES
    n_conv_lru = w_lru_conv.shape[1]
    n_conv_ffn = w_ffn_conv.shape[1]

    c_all = jnp.concatenate([c_sample, c_prompt, jnp.zeros((8, d), f32)], axis=0)
    mod = _ada(c_all, w_ada, b_ada.reshape(depth, 1, 6 * d), 1536)
    mod_kv = _ada(c_all, w_ada_kv[None], b_ada_kv.reshape(1, 1, 2 * d), 1024)
    prompt_rows = n_bs // n_bp

    win16 = w_lru_in.astype(bf16)
    wg16 = jnp.concatenate([w_gate_x, w_gate_a], axis=-1).astype(bf16)
    wout16 = w_lru_out.astype(bf16)
    wkv16 = w_kv.astype(bf16)
    wq16 = w_q.astype(bf16)
    wo16 = w_o.astype(bf16)
    wup16 = w_up.astype(bf16)
    wdown16 = w_down.astype(bf16)
    rows3 = lambda a: a[:, None, :]
    slopes_np = (2.0 ** (-8.0 * np.arange(1, n_heads + 1, dtype=np.float64) / n_heads)).astype(np.float32)
    slopes = jnp.asarray(slopes_np)
    head_scale = head_dim ** -0.5

    def run_group(x, group, mod_rows, steps_lru, steps_ffn, steps_proj, lru_h0, lru_conv0, ffn_conv0, attend,
                  batch_major):
        lru_h, lru_conv, ffn_conv = [], [], []
        kv = None
        for l in range(depth):
            if l < n_a:
                x, h_last, conv_new = _lru_layer(
                    x, mod, mod_rows, l, group, steps_lru, rows3(g_norm_mix), win16, w_lru_conv,
                    rows3(b_lru_conv), wg16, rows3(b_gate_x), rows3(b_gate_a),
                    rows3(lru_log_param), wout16, lru_h0[l], lru_conv0[l], batch_major and l == 0)
                lru_h.append(h_last)
                lru_conv.append(conv_new)
            else:
                j = l - n_a
                lam_init = 0.8 - 0.6 * math.exp(-0.3 * l)
                lam_params = jnp.stack([lam_q1[j], lam_k1[j], lam_q2[j], lam_k2[j]])
                gq2 = jnp.tile(g_q_norm[j], 2)[None]
                attn = attend(x, l, j, gq2, lam_params, g_subln[j][None], lam_init, kv)
                x = _o_proj(attn, x, mod, mod_rows, l, j, group, steps_proj, wo16, batch_major)
            x, fb = _ffn_layer(x, mod, mod_rows, l, group, steps_ffn, 1024, rows3(g_norm_ffn), wup16,
                               w_ffn_conv, rows3(b_ffn_conv), wdown16, ffn_conv0[l],
                               batch_major and l == depth - 1)
            ffn_conv.append(fb)
            if l == n_a - 1:
                kv = _kv_proj(x, mod_kv, mod_rows, group, steps_proj, g_norm_kv[None], wkv16,
                              jnp.tile(g_k_norm, 2)[None], batch_major)
        return x, kv[0], kv[1], lru_h, lru_conv, ffn_conv

    steps_proj_p = min(64, seq)

    def attend_prompt(x, l, j, gq2, lam_params, gsub, lam_init, kv):
        q = _q_proj(x, mod, prompt_rows, l, j, n_bp, steps_proj_p, rows3(g_norm_mix), wq16, gq2,
                    head_scale * LOG2E, bf16, True)
        return _prompt_attention(q, kv[2], kv[3], slopes, lam_params, gsub, lam_init, n_heads, min(512, seq))

    zeros_p = lambda n, r, w: [jnp.zeros((r, w), f32)] * n
    y_prompt, kp, vp, hp, lcp, fcp = run_group(
        x_prompt, n_bp, prompt_rows, min(64, seq), min(128, seq), steps_proj_p,
        zeros_p(n_a, n_bp, d_lru), zeros_p(n_a, (n_conv_lru - 1) * n_bp, d_lru),
        zeros_p(depth, (n_conv_ffn - 1) * n_bp, 2 * d_ff), attend_prompt, True)
    k_prompt = kp.reshape(n_bp, seq, n_heads, 2, head_dim)
    v_prompt = vp.reshape(n_bp, seq, n_heads, 2 * head_dim)
    state_bm = lambda s: jnp.stack([_batch_major(a, n_bp) for a in s])
    h_p = jnp.stack(hp)
    lc_p = state_bm(lcp)
    fc_p = state_bm(fcp)

    def attend_sample(x, l, j, gq2, lam_params, gsub, lam_init, kv):
        q = _q_proj(x, mod, 0, l, j, n_bs, 1, rows3(g_norm_mix), wq16, gq2, head_scale, f32, False)
        o = _decode_attention(q, kv[0], kv[1], cache_k, cache_v, page_table, slopes_np,
                              lam_params, gsub, lam_init, min(16, page_table.shape[1]))
        return o.reshape(n_bs, attn_w)

    xs = x_sample.reshape(n_bs, d)
    ys, ks, vs, hs, lcs, fcs = run_group(
        xs, n_bs, 0, 1, 1, 1,
        [state_lru_h[l] for l in range(n_a)],
        [_time_major(state_lru_conv[l]) for l in range(n_a)],
        [_time_major(state_ffn_conv[l]) for l in range(depth)], attend_sample, False)
    y_sample = ys.reshape(n_bs, 1, d)
    k_sample = ks.reshape(n_bs, 1, n_heads, 2, head_dim)
    v_sample = vs.reshape(n_bs, 1, n_heads, 2 * head_dim)
    state_bs = lambda s: jnp.stack([_batch_major(a, n_bs) for a in s])
    h_s = jnp.stack(hs)
    lc_s = state_bs(lcs)
    fc_s = state_bs(fcs)

    return (y_prompt, y_sample, k_prompt, v_prompt, k_sample, v_sample,
            h_p, h_s, lc_p, lc_s, fc_p, fc_s)
```

```python
import functools
import math

import numpy as np
import jax
import jax.numpy as jnp
from jax import lax
from jax.experimental import pallas as pl
from jax.experimental.pallas import tpu as pltpu

f32 = jnp.float32
bf16 = jnp.bfloat16

EPS = 1e-6
LRU_C = 8.0
LANES = 128
PAGE_SIZE = 128
MASKED = -1e30
LOG2E = math.log2(math.e)
VMEM_LIMIT_BYTES = 56 * 1024 * 1024


def _params(*semantics):
    return pltpu.CompilerParams(dimension_semantics=semantics,
                                vmem_limit_bytes=VMEM_LIMIT_BYTES)


def _const_spec(shape):
    zeros = (0,) * len(shape)
    return pl.BlockSpec(shape, lambda *_: zeros)


def _layer_spec(stacked_shape, layer):
    index = (layer,) + (0,) * (len(stacked_shape) - 1)
    return pl.BlockSpec((None,) + tuple(stacked_shape[1:]), lambda *_: index)


def _sigmoid(x):
    return 1.0 / (1.0 + jnp.exp(-x))


def _gelu_tanh(x):
    return 0.5 * x * (1.0 + jnp.tanh(math.sqrt(2.0 / math.pi) * (x + 0.044715 * (x * x * x))))


def _mod_norm(x, gain, shift, scale, group):
    rows, d = x.shape
    y = x * lax.rsqrt(jnp.mean(x * x, axis=-1, keepdims=True) + EPS) * gain
    y = y.reshape(rows // group, group, d) * (1.0 + scale)[None] + shift[None]
    return y.reshape(rows, d)


def _gate(x, gate, group):
    rows, d = x.shape
    return (x.reshape(rows // group, group, d) * gate[None]).reshape(rows, d)


def _head_half_norm(y, gain2):
    lane = lax.broadcasted_iota(jnp.int32, y.shape, 1)
    low = lane < (LANES // 2)
    sq = y * y
    s_low = jnp.sum(jnp.where(low, sq, 0.0), axis=-1, keepdims=True)
    s_high = jnp.sum(jnp.where(low, 0.0, sq), axis=-1, keepdims=True)
    inv = lax.rsqrt(jnp.where(low, s_low, s_high) * (2.0 / LANES) + EPS)
    return y * inv * gain2


def _store_batch_major(o_ref, slabs_ref, group):
    steps = slabs_ref.shape[1] // group
    for g in range(group):
        for j in range(slabs_ref.shape[0]):
            o_ref[g, :, j * LANES:(j + 1) * LANES] = (
                slabs_ref.at[j][pl.ds(g, steps, stride=group), :].astype(o_ref.dtype))


def _load_step_major(a_ref, slabs_ref, group):
    steps = a_ref.shape[1]
    n_slabs = slabs_ref.shape[0]
    for g in range(group):
        for j in range(n_slabs):
            slabs_ref.at[j][pl.ds(g, steps, stride=group), :] = (
                a_ref[g, :, j * LANES:(j + 1) * LANES].astype(f32))
    return jnp.concatenate([slabs_ref[j] for j in range(n_slabs)], axis=-1)


def _ada_kernel(c_ref, w_ref, b_ref, o_ref):
    c = c_ref[...]
    sc = (c * _sigmoid(c)).astype(bf16)
    o_ref[0] = jnp.dot(sc, w_ref[0].astype(bf16), preferred_element_type=f32) + b_ref[0]


def _ada(c, w, b, tn):
    n_layers, d, n = w.shape
    m = c.shape[0]
    return pl.pallas_call(
        _ada_kernel,
        out_shape=jax.ShapeDtypeStruct((n_layers, m, n), f32),
        grid=(n_layers, n // tn),
        in_specs=[_const_spec((m, d)),
                  pl.BlockSpec((1, d, tn), lambda l, j: (l, 0, j)),
                  pl.BlockSpec((1, 1, tn), lambda l, j: (l, 0, j))],
        out_specs=pl.BlockSpec((1, m, tn), lambda l, j: (l, 0, j)),
        compiler_params=_params("arbitrary", "arbitrary"),
        name="ada_mod",
    )(c, w, b)


def _mod_spec(group, row_block, layer, piece, d):
    return pl.BlockSpec((1, group, d), lambda *_: (layer, row_block, piece))


def _lru_kernel(group, steps, n_conv, batch_major_in,
                x_ref, sh_ref, sc_ref, gt_ref, gn_ref, win_ref, cw_ref, cb_ref,
                wg_ref, bgx_ref, bga_ref, lp_ref, wout_ref, h0_ref, c0_ref,
                o_ref, hlast_ref, cnew_ref,
                xw_scr, halo_scr, a_scr, b_scr, gb_scr, y_scr, h_scr, *slabs):
    rows = group * steps
    d_lru = a_scr.shape[1]
    halo = (n_conv - 1) * group
    t = pl.program_id(0)

    @pl.when(t == 0)
    def _():
        h_scr[...] = h0_ref[...]
        halo_scr[...] = c0_ref[...]

    x = _load_step_major(x_ref, slabs[0], group) if batch_major_in else x_ref[...]
    h = _mod_norm(x, gn_ref[...], sh_ref[0], sc_ref[0], group).astype(bf16)
    u = jnp.dot(h, win_ref[...], preferred_element_type=f32)
    gb_scr[...] = u[:, d_lru:]

    xw_scr[0:halo, :] = halo_scr[...]
    xw_scr[halo:halo + rows, :] = u[:, :d_lru]
    xc = cb_ref[...] + cw_ref[0:1, :] * xw_scr[0:rows, :]
    for j in range(1, n_conv):
        xc = xc + cw_ref[j:j + 1, :] * xw_scr[j * group:j * group + rows, :]
    new_halo = xw_scr[rows:rows + halo, :]
    halo_scr[...] = new_halo
    cnew_ref[...] = new_halo

    lp = lp_ref[...]
    log_sig = jnp.minimum(lp, 0.0) - jnp.log1p(jnp.exp(-jnp.abs(lp)))
    n_blocks = wg_ref.shape[0]
    blk = d_lru // n_blocks
    for n in range(n_blocks):
        cols = slice(n * blk, (n + 1) * blk)
        xn = xc[:, cols]
        gates = jnp.dot(xn.astype(bf16), wg_ref[n], preferred_element_type=f32)
        gate_x = _sigmoid(gates[:, :blk] + bgx_ref[:, cols])
        gate_a = _sigmoid(gates[:, blk:] + bga_ref[:, cols])
        log_a = LRU_C * gate_a * log_sig[:, cols]
        a = jnp.exp(log_a)
        a_scr[:, cols] = a
        b_scr[:, cols] = jnp.sqrt(-jnp.tanh(log_a) * (a * a + 1.0)) * (gate_x * xn)

    def step(i, hprev):
        r = pl.multiple_of(i * group, group)
        hnew = a_scr[pl.ds(r, group), :] * hprev + b_scr[pl.ds(r, group), :]
        b_scr[pl.ds(r, group), :] = hnew
        return hnew

    if steps == 1:
        hlast = step(0, h_scr[...])
    else:
        hlast = lax.fori_loop(0, steps, step, h_scr[...], unroll=8)
    h_scr[...] = hlast
    hlast_ref[...] = hlast

    y_scr[...] = (b_scr[...] * _gelu_tanh(gb_scr[...])).astype(bf16)
    out = jnp.dot(y_scr[...], wout_ref[...], preferred_element_type=f32)
    o_ref[...] = x + _gate(out, gt_ref[0], group)


def _lru_layer(x, mod, mod_rows, layer, group, steps, gn, win, cw, cb, wg, bgx, bga, lp, wout, h0, c0,
               batch_major_in):
    d = x.shape[-1]
    rows_total = x.shape[0] * x.shape[1] if batch_major_in else x.shape[0]
    d_lru = wout.shape[1]
    n_conv = cw.shape[1]
    rows = group * steps
    halo = (n_conv - 1) * group
    spec = lambda p: _mod_spec(group, mod_rows, layer, p, d)
    par = lambda a: _layer_spec(a.shape, layer)
    tile = pl.BlockSpec((rows, d), lambda t: (t, 0))
    if batch_major_in:
        x_spec = pl.BlockSpec((group, steps, d), lambda t: (0, t, 0))
        slabs = [pltpu.VMEM((d // LANES, rows, LANES), f32)]
    else:
        x_spec, slabs = tile, []
    kernel = functools.partial(_lru_kernel, group, steps, n_conv, batch_major_in)
    return pl.pallas_call(
        kernel,
        out_shape=(jax.ShapeDtypeStruct((rows_total, d), f32),
                   jax.ShapeDtypeStruct((group, d_lru), f32),
                   jax.ShapeDtypeStruct((halo, d_lru), f32)),
        grid=(rows_total // rows,),
        in_specs=[x_spec, spec(0), spec(1), spec(2), par(gn), par(win), par(cw), par(cb), par(wg), par(bgx),
                  par(bga), par(lp), par(wout), _const_spec(h0.shape), _const_spec(c0.shape)],
        out_specs=(tile, _const_spec((group, d_lru)), _const_spec((halo, d_lru))),
        scratch_shapes=[pltpu.VMEM((rows + halo, d_lru), f32), pltpu.VMEM((halo, d_lru), f32),
                        pltpu.VMEM((rows, d_lru), f32), pltpu.VMEM((rows, d_lru), f32),
                        pltpu.VMEM((rows, d_lru), f32), pltpu.VMEM((rows, d_lru), bf16),
                        pltpu.VMEM((group, d_lru), f32)] + slabs,
        compiler_params=_params("arbitrary"),
        name="rglru_layer",
    )(x, mod, mod, mod, gn, win, cw, cb, wg, bgx, bga, lp, wout, h0, c0)


def _ffn_kernel(group, steps, n_conv, fc, batch_major_out,
                x_ref, sh_ref, sc_ref, gt_ref, gn_ref, wup_ref, cw_ref, cb_ref, wd_ref, c0_ref,
                o_ref, cn_ref,
                uwa_scr, uwg_scr, halo_scr, *slabs):
    rows = group * steps
    halo = (n_conv - 1) * group
    d_ff = wd_ref.shape[0]

    @pl.when(pl.program_id(0) == 0)
    def _():
        halo_scr[...] = c0_ref[...]

    x = x_ref[...]
    h = _mod_norm(x, gn_ref[...], sh_ref[0], sc_ref[0], group).astype(bf16)

    def conv_half(col0, uw_scr):
        cols = slice(col0, col0 + fc)
        uw_scr[0:halo, :] = halo_scr[:, cols]
        uw_scr[halo:halo + rows, :] = jnp.dot(h, wup_ref[:, cols], preferred_element_type=f32)
        y = cb_ref[:, cols] + cw_ref[0:1, cols] * uw_scr[0:rows, :]
        for j in range(1, n_conv):
            y = y + cw_ref[j:j + 1, cols] * uw_scr[j * group:j * group + rows, :]
        new_halo = uw_scr[rows:rows + halo, :]
        halo_scr[:, cols] = new_halo
        cn_ref[:, cols] = new_halo
        return y

    acc = jnp.zeros(x.shape, f32)
    for c in range(d_ff // fc):
        a = conv_half(c * fc, uwa_scr)
        g = conv_half(d_ff + c * fc, uwg_scr)
        act = (_gelu_tanh(a) * g).astype(bf16)
        acc = acc + jnp.dot(act, wd_ref[c * fc:(c + 1) * fc, :], preferred_element_type=f32)

    y = x + _gate(acc, gt_ref[0], group)
    if batch_major_out:
        for j in range(slabs[0].shape[0]):
            slabs[0][j] = y[:, j * LANES:(j + 1) * LANES]
        _store_batch_major(o_ref, slabs[0], group)
    else:
        o_ref[...] = y


def _ffn_layer(x, mod, mod_rows, layer, group, steps, fc, gn, wup, cw, cb, wdown, c0, batch_major_out):
    rows_total, d = x.shape
    d_ff = wdown.shape[1]
    n_conv = cw.shape[1]
    rows = group * steps
    halo = (n_conv - 1) * group
    spec = lambda p: _mod_spec(group, mod_rows, layer, p, d)
    tile = pl.BlockSpec((rows, d), lambda t: (t, 0))
    resident = lambda a: pl.BlockSpec((None,) + tuple(a.shape[1:]), lambda t: (layer,) + (0,) * (a.ndim - 1),
                                      pipeline_mode=pl.Buffered(1))
    state = pl.BlockSpec((halo, 2 * d_ff), lambda t: (0, 0))
    if batch_major_out:
        out_shape = jax.ShapeDtypeStruct((group, rows_total // group, d), f32)
        out_spec = pl.BlockSpec((group, steps, d), lambda t: (0, t, 0))
        slabs = [pltpu.VMEM((d // LANES, rows, LANES), f32)]
    else:
        out_shape, out_spec, slabs = jax.ShapeDtypeStruct((rows_total, d), f32), tile, []
    kernel = functools.partial(_ffn_kernel, group, steps, n_conv, fc, batch_major_out)
    return pl.pallas_call(
        kernel,
        out_shape=(out_shape, jax.ShapeDtypeStruct((halo, 2 * d_ff), f32)),
        grid=(rows_total // rows,),
        in_specs=[tile, spec(3), spec(4), spec(5), _layer_spec(gn.shape, layer),
                  resident(wup), resident(cw), resident(cb), resident(wdown), state],
        out_specs=(out_spec, state),
        scratch_shapes=[pltpu.VMEM((rows + halo, fc), f32), pltpu.VMEM((rows + halo, fc), f32),
                        pltpu.VMEM((halo, 2 * d_ff), f32)] + slabs,
        compiler_params=_params("arbitrary"),
        name="conv_ffn",
    )(x, mod, mod, mod, gn, wup, cw, cb, wdown, c0)


def _q_kernel(group, scale, batch_major, x_ref, sh_ref, sc_ref, gn_ref, w_ref, gq_ref, o_ref, *slabs):
    h = _mod_norm(x_ref[...], gn_ref[...], sh_ref[0], sc_ref[0], group).astype(bf16)
    q = jnp.dot(h, w_ref[...], preferred_element_type=f32)
    for j in range(q.shape[1] // LANES):
        cols = slice(j * LANES, (j + 1) * LANES)
        qj = _head_half_norm(q[:, cols], gq_ref[...]) * scale
        if batch_major:
            slabs[0][j] = qj
        else:
            o_ref[:, cols] = qj.astype(o_ref.dtype)
    if batch_major:
        _store_batch_major(o_ref, slabs[0], group)


def _q_proj(x, mod, mod_rows, layer, block, group, steps, gn, wq, gq2, scale, out_dtype, batch_major):
    rows_total, d = x.shape
    width = wq.shape[2]
    rows = group * steps
    spec = lambda p: _mod_spec(group, mod_rows, layer, p, d)
    if batch_major:
        out_shape = jax.ShapeDtypeStruct((group, rows_total // group, width), out_dtype)
        out_spec = pl.BlockSpec((group, steps, width), lambda t: (0, t, 0))
        scratch = [pltpu.VMEM((width // LANES, rows, LANES), f32)]
    else:
        out_shape = jax.ShapeDtypeStruct((rows_total, width), out_dtype)
        out_spec = pl.BlockSpec((rows, width), lambda t: (t, 0))
        scratch = []
    return pl.pallas_call(
        functools.partial(_q_kernel, group, scale, batch_major),
        out_shape=out_shape,
        grid=(rows_total // rows,),
        in_specs=[pl.BlockSpec((rows, d), lambda t: (t, 0)), spec(0), spec(1), _layer_spec(gn.shape, layer),
                  _layer_spec(wq.shape, block), _const_spec(gq2.shape)],
        out_specs=out_spec,
        scratch_shapes=scratch,
        compiler_params=_params("arbitrary"),
        name="q_proj",
    )(x, mod, mod, gn, wq, gq2)


def _kv_kernel(group, batch_major, x_ref, sh_ref, sc_ref, gn_ref, w_ref, gk_ref, k_ref, v_ref, *slabs):
    h = _mod_norm(x_ref[...], gn_ref[...], sh_ref[0], sc_ref[0], group).astype(bf16)
    kv = jnp.dot(h, w_ref[...], preferred_element_type=f32)
    width = kv.shape[1] // 2
    for j in range(width // LANES):
        cols = slice(j * LANES, (j + 1) * LANES)
        kj = _head_half_norm(kv[:, cols], gk_ref[...])
        vj = kv[:, width + j * LANES:width + (j + 1) * LANES]
        if batch_major:
            slabs[0][j] = kj
            slabs[1][j] = vj
        else:
            k_ref[:, cols] = kj
            v_ref[:, cols] = vj
    if batch_major:
        _store_batch_major(k_ref, slabs[0], group)
        _store_batch_major(v_ref, slabs[1], group)


def _kv_proj(x, mod_kv, mod_rows, group, steps, gn, wkv, gk2, batch_major):
    rows_total, d = x.shape
    width = wkv.shape[1] // 2
    rows = group * steps
    spec = lambda p: _mod_spec(group, mod_rows, 0, p, d)
    if batch_major:
        shape = (group, rows_total // group, width)
        out_tile = pl.BlockSpec((group, steps, width), lambda t: (0, t, 0))
        scratch = [pltpu.VMEM((width // LANES, rows, LANES), f32)] * 2
    else:
        shape = (rows_total, width)
        out_tile = pl.BlockSpec((rows, width), lambda t: (t, 0))
        scratch = []
    return pl.pallas_call(
        functools.partial(_kv_kernel, group, batch_major),
        out_shape=(jax.ShapeDtypeStruct(shape, f32), jax.ShapeDtypeStruct(shape, f32)),
        grid=(rows_total // rows,),
        in_specs=[pl.BlockSpec((rows, d), lambda t: (t, 0)), spec(0), spec(1), _const_spec(gn.shape),
                  _const_spec(wkv.shape), _const_spec(gk2.shape)],
        out_specs=(out_tile, out_tile),
        scratch_shapes=scratch,
        compiler_params=_params("arbitrary"),
        name="kv_proj",
    )(x, mod_kv, mod_kv, gn, wkv, gk2)


def _o_kernel(group, batch_major, a_ref, x_ref, gt_ref, w_ref, o_ref, *slabs):
    if batch_major:
        a = _load_step_major(a_ref, slabs[0], group).astype(bf16)
    else:
        a = a_ref[...]
    out = jnp.dot(a, w_ref[...], preferred_element_type=f32)
    o_ref[...] = x_ref[...] + _gate(out, gt_ref[0], group)


def _o_proj(attn, x, mod, mod_rows, layer, block, group, steps, wo, batch_major):
    rows_total, d = x.shape
    width = wo.shape[1]
    rows = group * steps
    tile = pl.BlockSpec((rows, d), lambda t: (t, 0))
    if batch_major:
        a_spec = pl.BlockSpec((group, steps, width), lambda t: (0, t, 0))
        scratch = [pltpu.VMEM((width // LANES, rows, LANES), f32)]
    else:
        a_spec = pl.BlockSpec((rows, width), lambda t: (t, 0))
        scratch = []
    return pl.pallas_call(
        functools.partial(_o_kernel, group, batch_major),
        out_shape=jax.ShapeDtypeStruct((rows_total, d), f32),
        grid=(rows_total // rows,),
        in_specs=[a_spec, tile, _mod_spec(group, mod_rows, layer, 2, d), _layer_spec(wo.shape, block)],
        out_specs=tile,
        scratch_shapes=scratch,
        compiler_params=_params("arbitrary"),
        name="o_proj",
    )(attn, x, mod, wo)


def _lambda(lp, lam_init):
    s1 = jnp.sum(lp[0:1] * lp[1:2], axis=-1, keepdims=True)
    s2 = jnp.sum(lp[2:3] * lp[3:4], axis=-1, keepdims=True)
    return jnp.exp(s1) - jnp.exp(s2) + lam_init


def _sub_norm(o, gain, lam_init):
    return o * lax.rsqrt(jnp.mean(o * o, axis=-1, keepdims=True) + EPS) * gain * (1.0 - lam_init)


def _prompt_attn_kernel(lam_init, tile, slopes_ref, lp_ref, gs_ref, q_ref, k_ref, v_ref, o_ref,
                        bias_scr, k16_scr, v16_scr):
    head = pl.program_id(0)
    seq = pl.program_id(1)
    slope2 = slopes_ref[head] * LOG2E
    n_tiles = q_ref.shape[0] // tile

    @pl.when(seq == 0)
    def _():
        dist = (lax.broadcasted_iota(jnp.int32, (tile, tile), 0)
                - lax.broadcasted_iota(jnp.int32, (tile, tile), 1))
        for n in range(n_tiles):
            bias = -slope2 * (dist + n * tile).astype(f32)
            if n == 0:
                bias = jnp.where(dist >= 0, bias, MASKED)
            for c in range(2):
                bias_scr[n, c * tile:(c + 1) * tile, :] = bias

    k16_scr[...] = k_ref[...].astype(bf16)
    v16_scr[...] = v_ref[...].astype(bf16)
    lam = _lambda(lp_ref[...], lam_init)
    for qi in range(n_tiles):
        q = q_ref[qi * tile:(qi + 1) * tile, :]
        lane = lax.broadcasted_iota(jnp.int32, q.shape, 1)
        zero = jnp.zeros_like(q)
        q_maps = jnp.concatenate([jnp.where(lane < LANES // 2, q, zero),
                                  jnp.where(lane < LANES // 2, zero, q)], axis=0)
        for j in range(qi + 1):
            kt = k16_scr[j * tile:(j + 1) * tile, :]
            vt = v16_scr[j * tile:(j + 1) * tile, :]
            s = lax.dot_general(q_maps, kt, (((1,), (1,)), ((), ())), preferred_element_type=f32)
            s = s + bias_scr[qi - j]
            row_max = jnp.max(s, axis=-1, keepdims=True)
            m_new = row_max if j == 0 else jnp.maximum(m, row_max)
            p = jnp.exp2(s - m_new)
            p_lanes = p[:, 0:LANES]
            for i in range(1, tile // LANES):
                p_lanes = p_lanes + p[:, i * LANES:(i + 1) * LANES]
            pv = jnp.dot(p.astype(bf16), vt, preferred_element_type=f32)
            if j == 0:
                l, acc = p_lanes, pv
            else:
                alpha = jnp.exp2(m - m_new)
                l = alpha * l + p_lanes
                acc = alpha * acc + pv
            m = m_new
        normed = acc / jnp.sum(l, axis=-1, keepdims=True)
        o = normed[:tile] - lam * normed[tile:]
        o_ref[qi * tile:(qi + 1) * tile, :] = _sub_norm(o, gs_ref[...], lam_init).astype(o_ref.dtype)


def _prompt_attention(q, k, v, slopes, lam_params, gsub, lam_init, n_heads, tile):
    n_seq, t_len, _ = q.shape
    spec = pl.BlockSpec((None, t_len, LANES), lambda h, g: (g, 0, h))
    return pl.pallas_call(
        functools.partial(_prompt_attn_kernel, lam_init, tile),
        out_shape=jax.ShapeDtypeStruct(q.shape, bf16),
        grid=(n_heads, n_seq),
        in_specs=[pl.BlockSpec(memory_space=pltpu.SMEM), _const_spec(lam_params.shape), _const_spec(gsub.shape),
                  spec, spec, spec],
        out_specs=spec,
        scratch_shapes=[pltpu.VMEM((t_len // tile, 2 * tile, tile), f32),
                        pltpu.VMEM((t_len, LANES), bf16), pltpu.VMEM((t_len, LANES), bf16)],
        compiler_params=_params("arbitrary", "arbitrary"),
        name="prompt_attention",
    )(slopes, lam_params, gsub, q, k, v)


def _decode_attn_kernel(lam_init, n_pages, past_len,
                        pt_ref, lp_ref, gs_ref, slope_ref, qt_ref, q_ref, kn_ref, vn_ref, *refs):
    k_refs = refs[:n_pages]
    v_refs = refs[n_pages:2 * n_pages]
    o_ref = refs[2 * n_pages]
    qcol_scr, s_scr = refs[2 * n_pages + 1:]
    b = pl.program_id(0)
    _, n_heads, n_maps, head_dim, _ = k_refs[0].shape

    lane = lax.broadcasted_iota(jnp.int32, qt_ref.shape, 1)
    col = jnp.sum(jnp.where(lane == b, qt_ref[...], 0.0), axis=-1, keepdims=True)
    qcol_scr[...] = jnp.broadcast_to(col, qcol_scr.shape)

    for i in range(n_pages):
        for h in range(n_heads):
            for c in range(n_maps):
                r = (h * n_maps + c) * head_dim
                prod = k_refs[i][0, h, c] * qcol_scr[r:r + head_dim, :]
                s_scr[c * n_heads + h:c * n_heads + h + 1, i * PAGE_SIZE:(i + 1) * PAGE_SIZE] = (
                    jnp.sum(prod, axis=0, keepdims=True))

    s = s_scr[...]
    pos = lax.broadcasted_iota(jnp.int32, s.shape, 1)
    s = s - slope_ref[...] * (past_len - pos).astype(f32)
    prod = q_ref[0] * kn_ref[0]
    low = lax.broadcasted_iota(jnp.int32, prod.shape, 1) < head_dim
    s_self = jnp.concatenate([jnp.sum(jnp.where(low, prod, 0.0), axis=-1, keepdims=True),
                              jnp.sum(jnp.where(low, 0.0, prod), axis=-1, keepdims=True)], axis=0)
    m = jnp.maximum(jnp.max(s, axis=-1, keepdims=True), s_self)
    p = jnp.exp(s - m)
    p_self = jnp.exp(s_self - m)
    inv = 1.0 / (jnp.sum(p, axis=-1, keepdims=True) + p_self)
    lam = _lambda(lp_ref[...], lam_init)
    w = p[:n_heads] * inv[:n_heads] - lam * (p[n_heads:] * inv[n_heads:])
    w_self = p_self[:n_heads] * inv[:n_heads] - lam * (p_self[n_heads:] * inv[n_heads:])

    lane = lax.broadcasted_iota(jnp.int32, (n_heads, PAGE_SIZE), 1)
    ones = jnp.ones((PAGE_SIZE, LANES), bf16)
    acc = w_self * vn_ref[0]
    for i in range(n_pages):
        w_page = w[:, i * PAGE_SIZE:(i + 1) * PAGE_SIZE]
        one_token = jnp.concatenate([jnp.where(lane == t, w_page, 0.0) for t in range(PAGE_SIZE)], axis=0)
        spread = jnp.dot(one_token.astype(bf16), ones, preferred_element_type=f32)
        acc = acc + jnp.sum(spread.reshape(PAGE_SIZE, n_heads, LANES) * v_refs[i][...], axis=0)
    o_ref[0] = _sub_norm(acc, gs_ref[...], lam_init).astype(o_ref.dtype)


def _decode_attention(q, k_new, v_new, cache_k, cache_v, page_table, slopes_np, lam_params, gsub, lam_init):
    n_b, width = q.shape
    n_pool, page, n_heads, n_maps, head_dim = cache_k.shape
    n_pg = page_table.shape[1]
    past_len = n_pg * page
    head_w = n_maps * head_dim
    k_pages = jnp.transpose(cache_k, (0, 2, 3, 4, 1))
    v_tokens = cache_v.reshape(n_pool * page, n_heads, head_w)
    heads = lambda a: a.reshape(n_b, n_heads, head_w)
    slope_rows = jnp.asarray(np.tile(slopes_np, n_maps)[:, None], f32)
    pt_flat = page_table.reshape(-1)
    consts = [lam_params, gsub, slope_rows, q.T]
    sample = pl.BlockSpec((1, n_heads, head_w), lambda b, pt: (b, 0, 0))
    key_page = lambda i: pl.BlockSpec((1, n_heads, n_maps, head_dim, page),
                                      lambda b, pt: (pt[b * n_pg + i], 0, 0, 0, 0))
    value_page = lambda i: pl.BlockSpec((page, n_heads, head_w), lambda b, pt: (pt[b * n_pg + i], 0, 0))
    grid_spec = pltpu.PrefetchScalarGridSpec(
        num_scalar_prefetch=1,
        grid=(n_b,),
        in_specs=[pl.BlockSpec(a.shape, lambda b, pt, n=a.ndim: (0,) * n) for a in consts]
                 + [sample, sample, sample]
                 + [key_page(i) for i in range(n_pg)]
                 + [value_page(i) for i in range(n_pg)],
        out_specs=sample,
        scratch_shapes=[pltpu.VMEM((width, LANES), f32),
                        pltpu.VMEM((n_maps * n_heads, past_len), f32)],
    )
    return pl.pallas_call(
        functools.partial(_decode_attn_kernel, lam_init, n_pg, past_len),
        out_shape=jax.ShapeDtypeStruct((n_b, n_heads, head_w), bf16),
        grid_spec=grid_spec,
        compiler_params=_params("arbitrary"),
        name="decode_attention",
    )(pt_flat, *consts, heads(q), heads(k_new), heads(v_new), *([k_pages] * n_pg), *([v_tokens] * n_pg))


def _time_major(a):
    return jnp.swapaxes(a, 0, 1).reshape((a.shape[0] * a.shape[1],) + a.shape[2:])


def _batch_major(a, n_seq):
    return jnp.swapaxes(a.reshape((a.shape[0] // n_seq, n_seq) + a.shape[1:]), 0, 1)


def kernel(x_prompt, x_sample, c_prompt, c_sample, cache_k, cache_v, page_table, state_lru_h, state_lru_conv, state_ffn_conv, w_ada, b_ada, g_norm_mix, g_norm_ffn, w_lru_in, w_lru_conv, b_lru_conv, w_gate_x, b_gate_x, w_gate_a, b_gate_a, lru_log_param, w_lru_out, w_ada_kv, b_ada_kv, g_norm_kv, w_kv, g_k_norm, w_q, g_q_norm, lam_q1, lam_k1, lam_q2, lam_k2, g_subln, w_o, w_up, w_ffn_conv, b_ffn_conv, w_down):
    depth, d, _ = w_ada.shape
    n_a = w_lru_in.shape[0]
    n_bp, seq, _ = x_prompt.shape
    n_bs, dec_seq, _ = x_sample.shape
    assert dec_seq == 1 and n_bp == 8 and n_bs % 8 == 0 and n_bs <= LANES
    d_lru = w_lru_out.shape[1]
    d_ff = w_down.shape[1]
    head_dim = g_q_norm.shape[1]
    assert 2 * head_dim == LANES and cache_k.shape[1] == PAGE_SIZE
    attn_w = w_q.shape[2]
    n_heads = attn_w // LANES
    n_conv_lru = w_lru_conv.shape[1]
    n_conv_ffn = w_ffn_conv.shape[1]

    c_all = jnp.concatenate([c_sample, c_prompt, jnp.zeros((8, d), f32)], axis=0)
    mod = _ada(c_all, w_ada, b_ada.reshape(depth, 1, 6 * d), 1536)
    mod_kv = _ada(c_all, w_ada_kv[None], b_ada_kv.reshape(1, 1, 2 * d), 1024)
    prompt_rows = n_bs // n_bp

    win16 = w_lru_in.astype(bf16)
    wg16 = jnp.concatenate([w_gate_x, w_gate_a], axis=-1).astype(bf16)
    wout16 = w_lru_out.astype(bf16)
    wkv16 = w_kv.astype(bf16)
    wq16 = w_q.astype(bf16)
    wo16 = w_o.astype(bf16)
    wup16 = w_up.astype(bf16)
    wdown16 = w_down.astype(bf16)
    rows3 = lambda a: a[:, None, :]
    slopes_np = (2.0 ** (-8.0 * np.arange(1, n_heads + 1, dtype=np.float64) / n_heads)).astype(np.float32)
    slopes = jnp.asarray(slopes_np)
    head_scale = head_dim ** -0.5

    def run_group(x, group, mod_rows, steps_lru, steps_ffn, steps_proj, lru_h0, lru_conv0, ffn_conv0, attend,
                  batch_major):
        lru_h, lru_conv, ffn_conv = [], [], []
        kv = None
        for l in range(depth):
            if l < n_a:
                x, h_last, conv_new = _lru_layer(
                    x, mod, mod_rows, l, group, steps_lru, rows3(g_norm_mix), win16, w_lru_conv,
                    rows3(b_lru_conv), wg16, rows3(b_gate_x), rows3(b_gate_a),
                    rows3(lru_log_param), wout16, lru_h0[l], lru_conv0[l], batch_major and l == 0)
                lru_h.append(h_last)
                lru_conv.append(conv_new)
            else:
                j = l - n_a
                lam_init = 0.8 - 0.6 * math.exp(-0.3 * l)
                lam_params = jnp.stack([lam_q1[j], lam_k1[j], lam_q2[j], lam_k2[j]])
                gq2 = jnp.tile(g_q_norm[j], 2)[None]
                attn = attend(x, l, j, gq2, lam_params, g_subln[j][None], lam_init, kv)
                x = _o_proj(attn, x, mod, mod_rows, l, j, group, steps_proj, wo16, batch_major)
            x, fb = _ffn_layer(x, mod, mod_rows, l, group, steps_ffn, 1024, rows3(g_norm_ffn), wup16,
                               w_ffn_conv, rows3(b_ffn_conv), wdown16, ffn_conv0[l],
                               batch_major and l == depth - 1)
            ffn_conv.append(fb)
            if l == n_a - 1:
                kv = _kv_proj(x, mod_kv, mod_rows, group, steps_proj, g_norm_kv[None], wkv16,
                              jnp.tile(g_k_norm, 2)[None], batch_major)
        return x, kv[0], kv[1], lru_h, lru_conv, ffn_conv

    steps_proj_p = min(64, seq)

    def attend_prompt(x, l, j, gq2, lam_params, gsub, lam_init, kv):
        q = _q_proj(x, mod, prompt_rows, l, j, n_bp, steps_proj_p, rows3(g_norm_mix), wq16, gq2,
                    head_scale * LOG2E, bf16, True)
        return _prompt_attention(q, kv[0], kv[1], slopes, lam_params, gsub, lam_init, n_heads, min(512, seq))

    zeros_p = lambda n, r, w: [jnp.zeros((r, w), f32)] * n
    y_prompt, kp, vp, hp, lcp, fcp = run_group(
        x_prompt, n_bp, prompt_rows, min(64, seq), min(128, seq), steps_proj_p,
        zeros_p(n_a, n_bp, d_lru), zeros_p(n_a, (n_conv_lru - 1) * n_bp, d_lru),
        zeros_p(depth, (n_conv_ffn - 1) * n_bp, 2 * d_ff), attend_prompt, True)
    k_prompt = kp.reshape(n_bp, seq, n_heads, 2, head_dim)
    v_prompt = vp.reshape(n_bp, seq, n_heads, 2 * head_dim)
    state_bm = lambda s: jnp.stack([_batch_major(a, n_bp) for a in s])
    h_p = jnp.stack(hp)
    lc_p = state_bm(lcp)
    fc_p = state_bm(fcp)

    def attend_sample(x, l, j, gq2, lam_params, gsub, lam_init, kv):
        q = _q_proj(x, mod, 0, l, j, n_bs, 1, rows3(g_norm_mix), wq16, gq2, head_scale, f32, False)
        o = _decode_attention(q, kv[0], kv[1], cache_k, cache_v, page_table, slopes_np,
                              lam_params, gsub, lam_init)
        return o.reshape(n_bs, attn_w)

    xs = x_sample.reshape(n_bs, d)
    ys, ks, vs, hs, lcs, fcs = run_group(
        xs, n_bs, 0, 1, 1, 1,
        [state_lru_h[l] for l in range(n_a)],
        [_time_major(state_lru_conv[l]) for l in range(n_a)],
        [_time_major(state_ffn_conv[l]) for l in range(depth)], attend_sample, False)
    y_sample = ys.reshape(n_bs, 1, d)
    k_sample = ks.reshape(n_bs, 1, n_heads, 2, head_dim)
    v_sample = vs.reshape(n_bs, 1, n_heads, 2 * head_dim)
    state_bs = lambda s: jnp.stack([_batch_major(a, n_bs) for a in s])
    h_s = jnp.stack(hs)
    lc_s = state_bs(lcs)
    fc_s = state_bs(fcs)

    return (y_prompt, y_sample, k_prompt, v_prompt, k_sample, v_sample,
            h_p, h_s, lc_p, lc_s, fc_p, fc_s)
```

```python
import functools
import math

import numpy as np
import jax
import jax.numpy as jnp
from jax import lax
from jax.experimental import pallas as pl
from jax.experimental.pallas import tpu as pltpu

f32 = jnp.float32
bf16 = jnp.bfloat16

EPS = 1e-6
LRU_C = 8.0
LANES = 128
PAGE_SIZE = 128
MASKED = -1e30
LOG2E = math.log2(math.e)
VMEM_LIMIT_BYTES = 56 * 1024 * 1024


def _params(*semantics):
    return pltpu.CompilerParams(dimension_semantics=semantics,
                                vmem_limit_bytes=VMEM_LIMIT_BYTES)


def _const_spec(shape):
    zeros = (0,) * len(shape)
    return pl.BlockSpec(shape, lambda *_: zeros)


def _layer_spec(stacked_shape, layer):
    index = (layer,) + (0,) * (len(stacked_shape) - 1)
    return pl.BlockSpec((None,) + tuple(stacked_shape[1:]), lambda *_: index)


def _sigmoid(x):
    return 1.0 / (1.0 + jnp.exp(-x))


def _gelu_tanh(x):
    return 0.5 * x * (1.0 + jnp.tanh(math.sqrt(2.0 / math.pi) * (x + 0.044715 * (x * x * x))))


def _mod_norm(x, gain, shift, scale, group):
    rows, d = x.shape
    y = x * lax.rsqrt(jnp.mean(x * x, axis=-1, keepdims=True) + EPS) * gain
    y = y.reshape(rows // group, group, d) * (1.0 + scale)[None] + shift[None]
    return y.reshape(rows, d)


def _gate(x, gate, group):
    rows, d = x.shape
    return (x.reshape(rows // group, group, d) * gate[None]).reshape(rows, d)


def _head_half_norm(y, gain2):
    lane = lax.broadcasted_iota(jnp.int32, y.shape, 1)
    low = lane < (LANES // 2)
    sq = y * y
    s_low = jnp.sum(jnp.where(low, sq, 0.0), axis=-1, keepdims=True)
    s_high = jnp.sum(jnp.where(low, 0.0, sq), axis=-1, keepdims=True)
    inv = lax.rsqrt(jnp.where(low, s_low, s_high) * (2.0 / LANES) + EPS)
    return y * inv * gain2


def _store_batch_major(o_ref, slabs_ref, group):
    steps = slabs_ref.shape[1] // group
    for g in range(group):
        for j in range(slabs_ref.shape[0]):
            o_ref[g, :, j * LANES:(j + 1) * LANES] = (
                slabs_ref.at[j][pl.ds(g, steps, stride=group), :].astype(o_ref.dtype))


def _load_step_major(a_ref, slabs_ref, group):
    steps = a_ref.shape[1]
    n_slabs = slabs_ref.shape[0]
    for g in range(group):
        for j in range(n_slabs):
            slabs_ref.at[j][pl.ds(g, steps, stride=group), :] = (
                a_ref[g, :, j * LANES:(j + 1) * LANES].astype(f32))
    return jnp.concatenate([slabs_ref[j] for j in range(n_slabs)], axis=-1)


def _ada_kernel(c_ref, w_ref, b_ref, o_ref):
    c = c_ref[...]
    sc = (c * _sigmoid(c)).astype(bf16)
    o_ref[0] = jnp.dot(sc, w_ref[0].astype(bf16), preferred_element_type=f32) + b_ref[0]


def _ada(c, w, b, tn):
    n_layers, d, n = w.shape
    m = c.shape[0]
    return pl.pallas_call(
        _ada_kernel,
        out_shape=jax.ShapeDtypeStruct((n_layers, m, n), f32),
        grid=(n_layers, n // tn),
        in_specs=[_const_spec((m, d)),
                  pl.BlockSpec((1, d, tn), lambda l, j: (l, 0, j)),
                  pl.BlockSpec((1, 1, tn), lambda l, j: (l, 0, j))],
        out_specs=pl.BlockSpec((1, m, tn), lambda l, j: (l, 0, j)),
        compiler_params=_params("arbitrary", "arbitrary"),
        name="ada_mod",
    )(c, w, b)


def _mod_spec(group, row_block, layer, piece, d):
    return pl.BlockSpec((1, group, d), lambda *_: (layer, row_block, piece))


def _lru_kernel(group, steps, n_conv, batch_major_in,
                x_ref, sh_ref, sc_ref, gt_ref, gn_ref, win_ref, cw_ref, cb_ref,
                wg_ref, bgx_ref, bga_ref, lp_ref, wout_ref, h0_ref, c0_ref,
                o_ref, hlast_ref, cnew_ref,
                xw_scr, halo_scr, a_scr, b_scr, gb_scr, y_scr, h_scr, *slabs):
    rows = group * steps
    d_lru = a_scr.shape[1]
    halo = (n_conv - 1) * group
    t = pl.program_id(0)

    @pl.when(t == 0)
    def _():
        h_scr[...] = h0_ref[...]
        for j in range(n_conv - 1):
            halo_scr[j * group:(j + 1) * group, :] = c0_ref[:, j * d_lru:(j + 1) * d_lru]

    x = _load_step_major(x_ref, slabs[0], group) if batch_major_in else x_ref[...]
    h = _mod_norm(x, gn_ref[...], sh_ref[0], sc_ref[0], group).astype(bf16)
    u = jnp.dot(h, win_ref[...], preferred_element_type=f32)
    gb_scr[...] = u[:, d_lru:]

    xw_scr[0:halo, :] = halo_scr[...]
    xw_scr[halo:halo + rows, :] = u[:, :d_lru]
    xc = cb_ref[...] + cw_ref[0:1, :] * xw_scr[0:rows, :]
    for j in range(1, n_conv):
        xc = xc + cw_ref[j:j + 1, :] * xw_scr[j * group:j * group + rows, :]
    new_halo = xw_scr[rows:rows + halo, :]
    halo_scr[...] = new_halo
    for j in range(n_conv - 1):
        cnew_ref[:, j * d_lru:(j + 1) * d_lru] = new_halo[j * group:(j + 1) * group, :]

    lp = lp_ref[...]
    log_sig = jnp.minimum(lp, 0.0) - jnp.log1p(jnp.exp(-jnp.abs(lp)))
    n_blocks = wg_ref.shape[0]
    blk = d_lru // n_blocks
    for n in range(n_blocks):
        cols = slice(n * blk, (n + 1) * blk)
        xn = xc[:, cols]
        gates = jnp.dot(xn.astype(bf16), wg_ref[n], preferred_element_type=f32)
        gate_x = _sigmoid(gates[:, :blk] + bgx_ref[:, cols])
        gate_a = _sigmoid(gates[:, blk:] + bga_ref[:, cols])
        log_a = LRU_C * gate_a * log_sig[:, cols]
        a = jnp.exp(log_a)
        a_scr[:, cols] = a
        b_scr[:, cols] = jnp.sqrt(-jnp.tanh(log_a) * (a * a + 1.0)) * (gate_x * xn)

    def step(i, hprev):
        r = pl.multiple_of(i * group, group)
        hnew = a_scr[pl.ds(r, group), :] * hprev + b_scr[pl.ds(r, group), :]
        b_scr[pl.ds(r, group), :] = hnew
        return hnew

    if steps == 1:
        hlast = step(0, h_scr[...])
    else:
        hlast = lax.fori_loop(0, steps, step, h_scr[...], unroll=8)
    h_scr[...] = hlast
    hlast_ref[...] = hlast

    y_scr[...] = (b_scr[...] * _gelu_tanh(gb_scr[...])).astype(bf16)
    out = jnp.dot(y_scr[...], wout_ref[...], preferred_element_type=f32)
    o_ref[...] = x + _gate(out, gt_ref[0], group)


def _lru_layer(x, mod, mod_rows, layer, group, steps, gn, win, cw, cb, wg, bgx, bga, lp, wout, h0, c0,
               batch_major_in):
    d = x.shape[-1]
    rows_total = x.shape[0] * x.shape[1] if batch_major_in else x.shape[0]
    d_lru = wout.shape[1]
    n_conv = cw.shape[1]
    rows = group * steps
    halo = (n_conv - 1) * group
    spec = lambda p: _mod_spec(group, mod_rows, layer, p, d)
    par = lambda a: _layer_spec(a.shape, layer)
    tile = pl.BlockSpec((rows, d), lambda t: (t, 0))
    if batch_major_in:
        x_spec = pl.BlockSpec((group, steps, d), lambda t: (0, t, 0))
        slabs = [pltpu.VMEM((d // LANES, rows, LANES), f32)]
    else:
        x_spec, slabs = tile, []
    kernel = functools.partial(_lru_kernel, group, steps, n_conv, batch_major_in)
    return pl.pallas_call(
        kernel,
        out_shape=(jax.ShapeDtypeStruct((rows_total, d), f32),
                   jax.ShapeDtypeStruct((group, d_lru), f32),
                   jax.ShapeDtypeStruct(c0.shape, f32)),
        grid=(rows_total // rows,),
        in_specs=[x_spec, spec(0), spec(1), spec(2), par(gn), par(win), par(cw), par(cb), par(wg), par(bgx),
                  par(bga), par(lp), par(wout), _const_spec(h0.shape), _const_spec(c0.shape)],
        out_specs=(tile, _const_spec((group, d_lru)), _const_spec(c0.shape)),
        scratch_shapes=[pltpu.VMEM((rows + halo, d_lru), f32), pltpu.VMEM((halo, d_lru), f32),
                        pltpu.VMEM((rows, d_lru), f32), pltpu.VMEM((rows, d_lru), f32),
                        pltpu.VMEM((rows, d_lru), f32), pltpu.VMEM((rows, d_lru), bf16),
                        pltpu.VMEM((group, d_lru), f32)] + slabs,
        compiler_params=_params("arbitrary"),
        name="rglru_layer",
    )(x, mod, mod, mod, gn, win, cw, cb, wg, bgx, bga, lp, wout, h0, c0)


def _ffn_kernel(group, steps, n_conv, fc, batch_major_out,
                x_ref, sh_ref, sc_ref, gt_ref, gn_ref, wup_ref, cw_ref, cb_ref, wd_ref, c0_ref,
                o_ref, cn_ref,
                uwa_scr, uwg_scr, halo_scr, *slabs):
    rows = group * steps
    halo = (n_conv - 1) * group
    d_ff = wd_ref.shape[0]

    @pl.when(pl.program_id(0) == 0)
    def _():
        for j in range(n_conv - 1):
            halo_scr[j * group:(j + 1) * group, :] = c0_ref[:, j * 2 * d_ff:(j + 1) * 2 * d_ff]

    x = x_ref[...]
    h = _mod_norm(x, gn_ref[...], sh_ref[0], sc_ref[0], group).astype(bf16)

    def conv_half(col0, uw_scr):
        cols = slice(col0, col0 + fc)
        uw_scr[0:halo, :] = halo_scr[:, cols]
        uw_scr[halo:halo + rows, :] = jnp.dot(h, wup_ref[:, cols], preferred_element_type=f32)
        y = cb_ref[:, cols] + cw_ref[0:1, cols] * uw_scr[0:rows, :]
        for j in range(1, n_conv):
            y = y + cw_ref[j:j + 1, cols] * uw_scr[j * group:j * group + rows, :]
        new_halo = uw_scr[rows:rows + halo, :]
        halo_scr[:, cols] = new_halo
        for j in range(n_conv - 1):
            cn_ref[:, j * 2 * d_ff + col0:j * 2 * d_ff + col0 + fc] = new_halo[j * group:(j + 1) * group, :]
        return y

    acc = jnp.zeros(x.shape, f32)
    for c in range(d_ff // fc):
        a = conv_half(c * fc, uwa_scr)
        g = conv_half(d_ff + c * fc, uwg_scr)
        act = (_gelu_tanh(a) * g).astype(bf16)
        acc = acc + jnp.dot(act, wd_ref[c * fc:(c + 1) * fc, :], preferred_element_type=f32)

    y = x + _gate(acc, gt_ref[0], group)
    if batch_major_out:
        for j in range(slabs[0].shape[0]):
            slabs[0][j] = y[:, j * LANES:(j + 1) * LANES]
        _store_batch_major(o_ref, slabs[0], group)
    else:
        o_ref[...] = y


def _ffn_layer(x, mod, mod_rows, layer, group, steps, fc, gn, wup, cw, cb, wdown, c0, batch_major_out):
    rows_total, d = x.shape
    d_ff = wdown.shape[1]
    n_conv = cw.shape[1]
    rows = group * steps
    halo = (n_conv - 1) * group
    spec = lambda p: _mod_spec(group, mod_rows, layer, p, d)
    tile = pl.BlockSpec((rows, d), lambda t: (t, 0))
    resident = lambda a: pl.BlockSpec((None,) + tuple(a.shape[1:]), lambda t: (layer,) + (0,) * (a.ndim - 1),
                                      pipeline_mode=pl.Buffered(1))
    state = pl.BlockSpec(c0.shape, lambda t: (0, 0))
    if batch_major_out:
        out_shape = jax.ShapeDtypeStruct((group, rows_total // group, d), f32)
        out_spec = pl.BlockSpec((group, steps, d), lambda t: (0, t, 0))
        slabs = [pltpu.VMEM((d // LANES, rows, LANES), f32)]
    else:
        out_shape, out_spec, slabs = jax.ShapeDtypeStruct((rows_total, d), f32), tile, []
    kernel = functools.partial(_ffn_kernel, group, steps, n_conv, fc, batch_major_out)
    return pl.pallas_call(
        kernel,
        out_shape=(out_shape, jax.ShapeDtypeStruct(c0.shape, f32)),
        grid=(rows_total // rows,),
        in_specs=[tile, spec(3), spec(4), spec(5), _layer_spec(gn.shape, layer),
                  resident(wup), resident(cw), resident(cb), resident(wdown), state],
        out_specs=(out_spec, state),
        scratch_shapes=[pltpu.VMEM((rows + halo, fc), f32), pltpu.VMEM((rows + halo, fc), f32),
                        pltpu.VMEM((halo, 2 * d_ff), f32)] + slabs,
        compiler_params=_params("arbitrary"),
        name="conv_ffn",
    )(x, mod, mod, mod, gn, wup, cw, cb, wdown, c0)


def _q_kernel(group, scale, batch_major, x_ref, sh_ref, sc_ref, gn_ref, w_ref, gq_ref, o_ref, *slabs):
    h = _mod_norm(x_ref[...], gn_ref[...], sh_ref[0], sc_ref[0], group).astype(bf16)
    q = jnp.dot(h, w_ref[...], preferred_element_type=f32)
    for j in range(q.shape[1] // LANES):
        cols = slice(j * LANES, (j + 1) * LANES)
        qj = _head_half_norm(q[:, cols], gq_ref[...]) * scale
        if batch_major:
            slabs[0][j] = qj
        else:
            o_ref[:, cols] = qj.astype(o_ref.dtype)
    if batch_major:
        _store_batch_major(o_ref, slabs[0], group)


def _q_proj(x, mod, mod_rows, layer, block, group, steps, gn, wq, gq2, scale, out_dtype, batch_major):
    rows_total, d = x.shape
    width = wq.shape[2]
    rows = group * steps
    spec = lambda p: _mod_spec(group, mod_rows, layer, p, d)
    if batch_major:
        out_shape = jax.ShapeDtypeStruct((group, rows_total // group, width), out_dtype)
        out_spec = pl.BlockSpec((group, steps, width), lambda t: (0, t, 0))
        scratch = [pltpu.VMEM((width // LANES, rows, LANES), f32)]
    else:
        out_shape = jax.ShapeDtypeStruct((rows_total, width), out_dtype)
        out_spec = pl.BlockSpec((rows, width), lambda t: (t, 0))
        scratch = []
    return pl.pallas_call(
        functools.partial(_q_kernel, group, scale, batch_major),
        out_shape=out_shape,
        grid=(rows_total // rows,),
        in_specs=[pl.BlockSpec((rows, d), lambda t: (t, 0)), spec(0), spec(1), _layer_spec(gn.shape, layer),
                  _layer_spec(wq.shape, block), _const_spec(gq2.shape)],
        out_specs=out_spec,
        scratch_shapes=scratch,
        compiler_params=_params("arbitrary"),
        name="q_proj",
    )(x, mod, mod, gn, wq, gq2)


def _kv_kernel(group, batch_major, x_ref, sh_ref, sc_ref, gn_ref, w_ref, gk_ref, k_ref, v_ref, *slabs):
    h = _mod_norm(x_ref[...], gn_ref[...], sh_ref[0], sc_ref[0], group).astype(bf16)
    kv = jnp.dot(h, w_ref[...], preferred_element_type=f32)
    width = kv.shape[1] // 2
    for j in range(width // LANES):
        cols = slice(j * LANES, (j + 1) * LANES)
        kj = _head_half_norm(kv[:, cols], gk_ref[...])
        vj = kv[:, width + j * LANES:width + (j + 1) * LANES]
        if batch_major:
            slabs[0][j] = kj
            slabs[1][j] = vj
        else:
            k_ref[:, cols] = kj
            v_ref[:, cols] = vj
    if batch_major:
        _store_batch_major(k_ref, slabs[0], group)
        _store_batch_major(v_ref, slabs[1], group)


def _kv_proj(x, mod_kv, mod_rows, group, steps, gn, wkv, gk2, batch_major):
    rows_total, d = x.shape
    width = wkv.shape[1] // 2
    rows = group * steps
    spec = lambda p: _mod_spec(group, mod_rows, 0, p, d)
    if batch_major:
        shape = (group, rows_total // group, width)
        out_tile = pl.BlockSpec((group, steps, width), lambda t: (0, t, 0))
        scratch = [pltpu.VMEM((width // LANES, rows, LANES), f32)] * 2
    else:
        shape = (rows_total, width)
        out_tile = pl.BlockSpec((rows, width), lambda t: (t, 0))
        scratch = []
    return pl.pallas_call(
        functools.partial(_kv_kernel, group, batch_major),
        out_shape=(jax.ShapeDtypeStruct(shape, f32), jax.ShapeDtypeStruct(shape, f32)),
        grid=(rows_total // rows,),
        in_specs=[pl.BlockSpec((rows, d), lambda t: (t, 0)), spec(0), spec(1), _const_spec(gn.shape),
                  _const_spec(wkv.shape), _const_spec(gk2.shape)],
        out_specs=(out_tile, out_tile),
        scratch_shapes=scratch,
        compiler_params=_params("arbitrary"),
        name="kv_proj",
    )(x, mod_kv, mod_kv, gn, wkv, gk2)


def _o_kernel(group, batch_major, a_ref, x_ref, gt_ref, w_ref, o_ref, *slabs):
    if batch_major:
        a = _load_step_major(a_ref, slabs[0], group).astype(bf16)
    else:
        a = a_ref[...]
    out = jnp.dot(a, w_ref[...], preferred_element_type=f32)
    o_ref[...] = x_ref[...] + _gate(out, gt_ref[0], group)


def _o_proj(attn, x, mod, mod_rows, layer, block, group, steps, wo, batch_major):
    rows_total, d = x.shape
    width = wo.shape[1]
    rows = group * steps
    tile = pl.BlockSpec((rows, d), lambda t: (t, 0))
    if batch_major:
        a_spec = pl.BlockSpec((group, steps, width), lambda t: (0, t, 0))
        scratch = [pltpu.VMEM((width // LANES, rows, LANES), f32)]
    else:
        a_spec = pl.BlockSpec((rows, width), lambda t: (t, 0))
        scratch = []
    return pl.pallas_call(
        functools.partial(_o_kernel, group, batch_major),
        out_shape=jax.ShapeDtypeStruct((rows_total, d), f32),
        grid=(rows_total // rows,),
        in_specs=[a_spec, tile, _mod_spec(group, mod_rows, layer, 2, d), _layer_spec(wo.shape, block)],
        out_specs=tile,
        scratch_shapes=scratch,
        compiler_params=_params("arbitrary"),
        name="o_proj",
    )(attn, x, mod, wo)


def _lambda(lp, lam_init):
    s1 = jnp.sum(lp[0:1] * lp[1:2], axis=-1, keepdims=True)
    s2 = jnp.sum(lp[2:3] * lp[3:4], axis=-1, keepdims=True)
    return jnp.exp(s1) - jnp.exp(s2) + lam_init


def _sub_norm(o, gain, lam_init):
    return o * lax.rsqrt(jnp.mean(o * o, axis=-1, keepdims=True) + EPS) * gain * (1.0 - lam_init)


def _prompt_attn_kernel(lam_init, tile, slopes_ref, lp_ref, gs_ref, q_ref, k_ref, v_ref, o_ref,
                        bias_scr, k16_scr, v16_scr):
    head = pl.program_id(0)
    seq = pl.program_id(1)
    slope2 = slopes_ref[head] * LOG2E
    n_tiles = q_ref.shape[0] // tile

    @pl.when(seq == 0)
    def _():
        dist = (lax.broadcasted_iota(jnp.int32, (tile, tile), 0)
                - lax.broadcasted_iota(jnp.int32, (tile, tile), 1))
        for n in range(n_tiles):
            bias = -slope2 * (dist + n * tile).astype(f32)
            if n == 0:
                bias = jnp.where(dist >= 0, bias, MASKED)
            for c in range(2):
                bias_scr[n, c * tile:(c + 1) * tile, :] = bias

    k16_scr[...] = k_ref[...].astype(bf16)
    v16_scr[...] = v_ref[...].astype(bf16)
    lam = _lambda(lp_ref[...], lam_init)
    for qi in range(n_tiles):
        q = q_ref[qi * tile:(qi + 1) * tile, :]
        lane = lax.broadcasted_iota(jnp.int32, q.shape, 1)
        zero = jnp.zeros_like(q)
        q_maps = jnp.concatenate([jnp.where(lane < LANES // 2, q, zero),
                                  jnp.where(lane < LANES // 2, zero, q)], axis=0)
        for j in range(qi + 1):
            kt = k16_scr[j * tile:(j + 1) * tile, :]
            vt = v16_scr[j * tile:(j + 1) * tile, :]
            s = lax.dot_general(q_maps, kt, (((1,), (1,)), ((), ())), preferred_element_type=f32)
            s = s + bias_scr[qi - j]
            row_max = jnp.max(s, axis=-1, keepdims=True)
            m_new = row_max if j == 0 else jnp.maximum(m, row_max)
            p = jnp.exp2(s - m_new)
            p_lanes = p[:, 0:LANES]
            for i in range(1, tile // LANES):
                p_lanes = p_lanes + p[:, i * LANES:(i + 1) * LANES]
            pv = jnp.dot(p.astype(bf16), vt, preferred_element_type=f32)
            if j == 0:
                l, acc = p_lanes, pv
            else:
                alpha = jnp.exp2(m - m_new)
                l = alpha * l + p_lanes
                acc = alpha * acc + pv
            m = m_new
        normed = acc / jnp.sum(l, axis=-1, keepdims=True)
        o = normed[:tile] - lam * normed[tile:]
        o_ref[qi * tile:(qi + 1) * tile, :] = _sub_norm(o, gs_ref[...], lam_init).astype(o_ref.dtype)


def _prompt_attention(q, k, v, slopes, lam_params, gsub, lam_init, n_heads, tile):
    n_seq, t_len, _ = q.shape
    spec = pl.BlockSpec((None, t_len, LANES), lambda h, g: (g, 0, h))
    return pl.pallas_call(
        functools.partial(_prompt_attn_kernel, lam_init, tile),
        out_shape=jax.ShapeDtypeStruct(q.shape, bf16),
        grid=(n_heads, n_seq),
        in_specs=[pl.BlockSpec(memory_space=pltpu.SMEM), _const_spec(lam_params.shape), _const_spec(gsub.shape),
                  spec, spec, spec],
        out_specs=spec,
        scratch_shapes=[pltpu.VMEM((t_len // tile, 2 * tile, tile), f32),
                        pltpu.VMEM((t_len, LANES), bf16), pltpu.VMEM((t_len, LANES), bf16)],
        compiler_params=_params("arbitrary", "arbitrary"),
        name="prompt_attention",
    )(slopes, lam_params, gsub, q, k, v)


def _decode_attn_kernel(lam_init, n_pages, past_len,
                        pt_ref, lp_ref, gs_ref, slope_ref, qt_ref, q_ref, kn_ref, vn_ref, *refs):
    k_refs = refs[:n_pages]
    v_refs = refs[n_pages:2 * n_pages]
    o_ref = refs[2 * n_pages]
    qcol_scr, s_scr = refs[2 * n_pages + 1:]
    b = pl.program_id(0)
    _, n_heads, n_maps, head_dim, _ = k_refs[0].shape

    lane = lax.broadcasted_iota(jnp.int32, qt_ref.shape, 1)
    col = jnp.sum(jnp.where(lane == b, qt_ref[...], 0.0), axis=-1, keepdims=True)
    qcol_scr[...] = jnp.broadcast_to(col, qcol_scr.shape)

    for i in range(n_pages):
        for h in range(n_heads):
            for c in range(n_maps):
                r = (h * n_maps + c) * head_dim
                prod = k_refs[i][0, h, c] * qcol_scr[r:r + head_dim, :]
                s_scr[c * n_heads + h:c * n_heads + h + 1, i * PAGE_SIZE:(i + 1) * PAGE_SIZE] = (
                    jnp.sum(prod, axis=0, keepdims=True))

    s = s_scr[...]
    pos = lax.broadcasted_iota(jnp.int32, s.shape, 1)
    s = s - slope_ref[...] * (past_len - pos).astype(f32)
    prod = q_ref[0] * kn_ref[0]
    low = lax.broadcasted_iota(jnp.int32, prod.shape, 1) < head_dim
    s_self = jnp.concatenate([jnp.sum(jnp.where(low, prod, 0.0), axis=-1, keepdims=True),
                              jnp.sum(jnp.where(low, 0.0, prod), axis=-1, keepdims=True)], axis=0)
    m = jnp.maximum(jnp.max(s, axis=-1, keepdims=True), s_self)
    p = jnp.exp(s - m)
    p_self = jnp.exp(s_self - m)
    inv = 1.0 / (jnp.sum(p, axis=-1, keepdims=True) + p_self)
    lam = _lambda(lp_ref[...], lam_init)
    w = p[:n_heads] * inv[:n_heads] - lam * (p[n_heads:] * inv[n_heads:])
    w_self = p_self[:n_heads] * inv[:n_heads] - lam * (p_self[n_heads:] * inv[n_heads:])

    lane = lax.broadcasted_iota(jnp.int32, (n_heads, PAGE_SIZE), 1)
    ones = jnp.ones((PAGE_SIZE, LANES), bf16)
    acc = w_self * vn_ref[0]
    for i in range(n_pages):
        w_page = w[:, i * PAGE_SIZE:(i + 1) * PAGE_SIZE]
        one_token = jnp.concatenate([jnp.where(lane == t, w_page, 0.0) for t in range(PAGE_SIZE)], axis=0)
        spread = jnp.dot(one_token.astype(bf16), ones, preferred_element_type=f32)
        acc = acc + jnp.sum(spread.reshape(PAGE_SIZE, n_heads, LANES) * v_refs[i][...], axis=0)
    o_ref[0] = _sub_norm(acc, gs_ref[...], lam_init).astype(o_ref.dtype)


def _decode_attention(q, k_new, v_new, cache_k, cache_v, page_table, slopes_np, lam_params, gsub, lam_init):
    n_b, width = q.shape
    n_pool, page, n_heads, n_maps, head_dim = cache_k.shape
    n_pg = page_table.shape[1]
    past_len = n_pg * page
    head_w = n_maps * head_dim
    k_pages = jnp.transpose(cache_k, (0, 2, 3, 4, 1))
    v_tokens = cache_v.reshape(n_pool * page, n_heads, head_w)
    heads = lambda a: a.reshape(n_b, n_heads, head_w)
    slope_rows = jnp.asarray(np.tile(slopes_np, n_maps)[:, None], f32)
    pt_flat = page_table.reshape(-1)
    consts = [lam_params, gsub, slope_rows, q.T]
    sample = pl.BlockSpec((1, n_heads, head_w), lambda b, pt: (b, 0, 0))
    key_page = lambda i: pl.BlockSpec((1, n_heads, n_maps, head_dim, page),
                                      lambda b, pt: (pt[b * n_pg + i], 0, 0, 0, 0))
    value_page = lambda i: pl.BlockSpec((page, n_heads, head_w), lambda b, pt: (pt[b * n_pg + i], 0, 0))
    grid_spec = pltpu.PrefetchScalarGridSpec(
        num_scalar_prefetch=1,
        grid=(n_b,),
        in_specs=[pl.BlockSpec(a.shape, lambda b, pt, n=a.ndim: (0,) * n) for a in consts]
                 + [sample, sample, sample]
                 + [key_page(i) for i in range(n_pg)]
                 + [value_page(i) for i in range(n_pg)],
        out_specs=sample,
        scratch_shapes=[pltpu.VMEM((width, LANES), f32),
                        pltpu.VMEM((n_maps * n_heads, past_len), f32)],
    )
    return pl.pallas_call(
        functools.partial(_decode_attn_kernel, lam_init, n_pg, past_len),
        out_shape=jax.ShapeDtypeStruct((n_b, n_heads, head_w), bf16),
        grid_spec=grid_spec,
        compiler_params=_params("arbitrary"),
        name="decode_attention",
    )(pt_flat, *consts, heads(q), heads(k_new), heads(v_new), *([k_pages] * n_pg), *([v_tokens] * n_pg))


def kernel(x_prompt, x_sample, c_prompt, c_sample, cache_k, cache_v, page_table, state_lru_h, state_lru_conv, state_ffn_conv, w_ada, b_ada, g_norm_mix, g_norm_ffn, w_lru_in, w_lru_conv, b_lru_conv, w_gate_x, b_gate_x, w_gate_a, b_gate_a, lru_log_param, w_lru_out, w_ada_kv, b_ada_kv, g_norm_kv, w_kv, g_k_norm, w_q, g_q_norm, lam_q1, lam_k1, lam_q2, lam_k2, g_subln, w_o, w_up, w_ffn_conv, b_ffn_conv, w_down):
    depth, d, _ = w_ada.shape
    n_a = w_lru_in.shape[0]
    n_bp, seq, _ = x_prompt.shape
    n_bs, dec_seq, _ = x_sample.shape
    assert dec_seq == 1 and n_bp == 8 and n_bs % 8 == 0 and n_bs <= LANES
    d_lru = w_lru_out.shape[1]
    d_ff = w_down.shape[1]
    head_dim = g_q_norm.shape[1]
    assert 2 * head_dim == LANES and cache_k.shape[1] == PAGE_SIZE
    attn_w = w_q.shape[2]
    n_heads = attn_w // LANES
    n_conv_lru = w_lru_conv.shape[1]
    n_conv_ffn = w_ffn_conv.shape[1]

    c_all = jnp.concatenate([c_sample, c_prompt, jnp.zeros((8, d), f32)], axis=0)
    mod = _ada(c_all, w_ada, b_ada.reshape(depth, 1, 6 * d), 1536)
    mod_kv = _ada(c_all, w_ada_kv[None], b_ada_kv.reshape(1, 1, 2 * d), 1024)
    prompt_rows = n_bs // n_bp

    win16 = w_lru_in.astype(bf16)
    wg16 = jnp.concatenate([w_gate_x, w_gate_a], axis=-1).astype(bf16)
    wout16 = w_lru_out.astype(bf16)
    wkv16 = w_kv.astype(bf16)
    wq16 = w_q.astype(bf16)
    wo16 = w_o.astype(bf16)
    wup16 = w_up.astype(bf16)
    wdown16 = w_down.astype(bf16)
    rows3 = lambda a: a[:, None, :]
    slopes_np = (2.0 ** (-8.0 * np.arange(1, n_heads + 1, dtype=np.float64) / n_heads)).astype(np.float32)
    slopes = jnp.asarray(slopes_np)
    head_scale = head_dim ** -0.5

    def run_group(x, group, mod_rows, steps_lru, steps_ffn, steps_proj, lru_h0, lru_conv0, ffn_conv0, attend,
                  batch_major):
        lru_h, lru_conv, ffn_conv = [], [], []
        kv = None
        for l in range(depth):
            if l < n_a:
                x, h_last, conv_new = _lru_layer(
                    x, mod, mod_rows, l, group, steps_lru, rows3(g_norm_mix), win16, w_lru_conv,
                    rows3(b_lru_conv), wg16, rows3(b_gate_x), rows3(b_gate_a),
                    rows3(lru_log_param), wout16, lru_h0[l], lru_conv0[l], batch_major and l == 0)
                lru_h.append(h_last)
                lru_conv.append(conv_new)
            else:
                j = l - n_a
                lam_init = 0.8 - 0.6 * math.exp(-0.3 * l)
                lam_params = jnp.stack([lam_q1[j], lam_k1[j], lam_q2[j], lam_k2[j]])
                gq2 = jnp.tile(g_q_norm[j], 2)[None]
                attn = attend(x, l, j, gq2, lam_params, g_subln[j][None], lam_init, kv)
                x = _o_proj(attn, x, mod, mod_rows, l, j, group, steps_proj, wo16, batch_major)
            x, fb = _ffn_layer(x, mod, mod_rows, l, group, steps_ffn, 1024, rows3(g_norm_ffn), wup16,
                               w_ffn_conv, rows3(b_ffn_conv), wdown16, ffn_conv0[l],
                               batch_major and l == depth - 1)
            ffn_conv.append(fb)
            if l == n_a - 1:
                kv = _kv_proj(x, mod_kv, mod_rows, group, steps_proj, g_norm_kv[None], wkv16,
                              jnp.tile(g_k_norm, 2)[None], batch_major)
        return x, kv[0], kv[1], lru_h, lru_conv, ffn_conv

    steps_proj_p = min(64, seq)

    def attend_prompt(x, l, j, gq2, lam_params, gsub, lam_init, kv):
        q = _q_proj(x, mod, prompt_rows, l, j, n_bp, steps_proj_p, rows3(g_norm_mix), wq16, gq2,
                    head_scale * LOG2E, bf16, True)
        return _prompt_attention(q, kv[0], kv[1], slopes, lam_params, gsub, lam_init, n_heads, min(256, seq))

    zeros_p = lambda n, r, w: [jnp.zeros((r, w), f32)] * n
    y_prompt, kp, vp, hp, lcp, fcp = run_group(
        x_prompt, n_bp, prompt_rows, min(64, seq), min(128, seq), steps_proj_p,
        zeros_p(n_a, n_bp, d_lru), zeros_p(n_a, n_bp, (n_conv_lru - 1) * d_lru),
        zeros_p(depth, n_bp, (n_conv_ffn - 1) * 2 * d_ff), attend_prompt, True)
    k_prompt = kp.reshape(n_bp, seq, n_heads, 2, head_dim)
    v_prompt = vp.reshape(n_bp, seq, n_heads, 2 * head_dim)
    h_p = jnp.stack(hp)
    lc_p = jnp.stack(lcp).reshape(n_a, n_bp, n_conv_lru - 1, d_lru)
    fc_p = jnp.stack(fcp).reshape(depth, n_bp, n_conv_ffn - 1, 2 * d_ff)

    def attend_sample(x, l, j, gq2, lam_params, gsub, lam_init, kv):
        q = _q_proj(x, mod, 0, l, j, n_bs, 1, rows3(g_norm_mix), wq16, gq2, head_scale, f32, False)
        o = _decode_attention(q, kv[0], kv[1], cache_k, cache_v, page_table, slopes_np,
                              lam_params, gsub, lam_init)
        return o.reshape(n_bs, attn_w)

    xs = x_sample.reshape(n_bs, d)
    ys, ks, vs, hs, lcs, fcs = run_group(
        xs, n_bs, 0, 1, 1, 1,
        [state_lru_h[l] for l in range(n_a)],
        [state_lru_conv[l].reshape(n_bs, -1) for l in range(n_a)],
        [state_ffn_conv[l].reshape(n_bs, -1) for l in range(depth)], attend_sample, False)
    y_sample = ys.reshape(n_bs, 1, d)
    k_sample = ks.reshape(n_bs, 1, n_heads, 2, head_dim)
    v_sample = vs.reshape(n_bs, 1, n_heads, 2 * head_dim)
    h_s = jnp.stack(hs)
    lc_s = jnp.stack(lcs).reshape(n_a, n_bs, n_conv_lru - 1, d_lru)
    fc_s = jnp.stack(fcs).reshape(depth, n_bs, n_conv_ffn - 1, 2 * d_ff)

    return (y_prompt, y_sample, k_prompt, v_prompt, k_sample, v_sample,
            h_p, h_s, lc_p, lc_s, fc_p, fc_s)
```

```python
import functools
import math

import numpy as np
import jax
import jax.numpy as jnp
from jax import lax
from jax.experimental import pallas as pl
from jax.experimental.pallas import tpu as pltpu

f32 = jnp.float32
bf16 = jnp.bfloat16

EPS = 1e-6
LRU_C = 8.0
LANES = 128
PAGE_SIZE = 128
MASKED = -1e30
LOG2E = math.log2(math.e)
VMEM_LIMIT_BYTES = 56 * 1024 * 1024


def _params(*semantics):
    return pltpu.CompilerParams(dimension_semantics=semantics,
                                vmem_limit_bytes=VMEM_LIMIT_BYTES)


def _const_spec(shape):
    zeros = (0,) * len(shape)
    return pl.BlockSpec(shape, lambda *_: zeros)


def _layer_spec(stacked_shape, layer):
    index = (layer,) + (0,) * (len(stacked_shape) - 1)
    return pl.BlockSpec((None,) + tuple(stacked_shape[1:]), lambda *_: index)


def _sigmoid(x):
    return 1.0 / (1.0 + jnp.exp(-x))


def _gelu_tanh(x):
    return 0.5 * x * (1.0 + jnp.tanh(math.sqrt(2.0 / math.pi) * (x + 0.044715 * (x * x * x))))


def _mod_norm(x, gain, shift, scale, group):
    rows, d = x.shape
    y = x * lax.rsqrt(jnp.mean(x * x, axis=-1, keepdims=True) + EPS) * gain
    y = y.reshape(rows // group, group, d) * (1.0 + scale)[None] + shift[None]
    return y.reshape(rows, d)


def _gate(x, gate, group):
    rows, d = x.shape
    return (x.reshape(rows // group, group, d) * gate[None]).reshape(rows, d)


def _head_half_norm(y, gain2):
    lane = lax.broadcasted_iota(jnp.int32, y.shape, 1)
    low = lane < (LANES // 2)
    sq = y * y
    s_low = jnp.sum(jnp.where(low, sq, 0.0), axis=-1, keepdims=True)
    s_high = jnp.sum(jnp.where(low, 0.0, sq), axis=-1, keepdims=True)
    inv = lax.rsqrt(jnp.where(low, s_low, s_high) * (2.0 / LANES) + EPS)
    return y * inv * gain2


def _store_batch_major(o_ref, slabs_ref, group):
    steps = slabs_ref.shape[1] // group
    for g in range(group):
        for j in range(slabs_ref.shape[0]):
            o_ref[g, :, j * LANES:(j + 1) * LANES] = (
                slabs_ref.at[j][pl.ds(g, steps, stride=group), :].astype(o_ref.dtype))


def _load_step_major(a_ref, slabs_ref, group):
    steps = a_ref.shape[1]
    n_slabs = slabs_ref.shape[0]
    for g in range(group):
        for j in range(n_slabs):
            slabs_ref.at[j][pl.ds(g, steps, stride=group), :] = (
                a_ref[g, :, j * LANES:(j + 1) * LANES].astype(f32))
    return jnp.concatenate([slabs_ref[j] for j in range(n_slabs)], axis=-1)


def _ada_kernel(c_ref, w_ref, b_ref, o_ref):
    c = c_ref[...]
    sc = (c * _sigmoid(c)).astype(bf16)
    o_ref[0] = jnp.dot(sc, w_ref[0].astype(bf16), preferred_element_type=f32) + b_ref[0]


def _ada(c, w, b, tn):
    n_layers, d, n = w.shape
    m = c.shape[0]
    return pl.pallas_call(
        _ada_kernel,
        out_shape=jax.ShapeDtypeStruct((n_layers, m, n), f32),
        grid=(n_layers, n // tn),
        in_specs=[_const_spec((m, d)),
                  pl.BlockSpec((1, d, tn), lambda l, j: (l, 0, j)),
                  pl.BlockSpec((1, 1, tn), lambda l, j: (l, 0, j))],
        out_specs=pl.BlockSpec((1, m, tn), lambda l, j: (l, 0, j)),
        compiler_params=_params("arbitrary", "arbitrary"),
        name="ada_mod",
    )(c, w, b)


def _mod_spec(group, row_block, layer, piece, d):
    return pl.BlockSpec((1, group, d), lambda *_: (layer, row_block, piece))


def _lru_kernel(group, steps, n_conv, batch_major_in,
                x_ref, sh_ref, sc_ref, gt_ref, gn_ref, win_ref, cw_ref, cb_ref,
                wg_ref, bgx_ref, bga_ref, lp_ref, wout_ref, h0_ref, c0_ref,
                o_ref, hlast_ref, cnew_ref,
                xw_scr, halo_scr, a_scr, b_scr, gb_scr, y_scr, h_scr, *slabs):
    rows = group * steps
    d_lru = a_scr.shape[1]
    halo = (n_conv - 1) * group
    t = pl.program_id(0)

    @pl.when(t == 0)
    def _():
        h_scr[...] = h0_ref[...]
        halo_scr[...] = c0_ref[...]

    x = _load_step_major(x_ref, slabs[0], group) if batch_major_in else x_ref[...]
    h = _mod_norm(x, gn_ref[...], sh_ref[0], sc_ref[0], group).astype(bf16)
    u = jnp.dot(h, win_ref[...], preferred_element_type=f32)
    gb_scr[...] = u[:, d_lru:]

    xw_scr[0:halo, :] = halo_scr[...]
    xw_scr[halo:halo + rows, :] = u[:, :d_lru]
    xc = cb_ref[...] + cw_ref[0:1, :] * xw_scr[0:rows, :]
    for j in range(1, n_conv):
        xc = xc + cw_ref[j:j + 1, :] * xw_scr[j * group:j * group + rows, :]
    new_halo = xw_scr[rows:rows + halo, :]
    halo_scr[...] = new_halo
    cnew_ref[...] = new_halo

    lp = lp_ref[...]
    log_sig = jnp.minimum(lp, 0.0) - jnp.log1p(jnp.exp(-jnp.abs(lp)))
    n_blocks = wg_ref.shape[0]
    blk = d_lru // n_blocks
    for n in range(n_blocks):
        cols = slice(n * blk, (n + 1) * blk)
        xn = xc[:, cols]
        gates = jnp.dot(xn.astype(bf16), wg_ref[n], preferred_element_type=f32)
        gate_x = _sigmoid(gates[:, :blk] + bgx_ref[:, cols])
        gate_a = _sigmoid(gates[:, blk:] + bga_ref[:, cols])
        log_a = LRU_C * gate_a * log_sig[:, cols]
        a = jnp.exp(log_a)
        a_scr[:, cols] = a
        b_scr[:, cols] = jnp.sqrt(-jnp.tanh(log_a) * (a * a + 1.0)) * (gate_x * xn)

    def step(i, hprev):
        r = pl.multiple_of(i * group, group)
        hnew = a_scr[pl.ds(r, group), :] * hprev + b_scr[pl.ds(r, group), :]
        b_scr[pl.ds(r, group), :] = hnew
        return hnew

    if steps == 1:
        hlast = step(0, h_scr[...])
    else:
        hlast = lax.fori_loop(0, steps, step, h_scr[...], unroll=8)
    h_scr[...] = hlast
    hlast_ref[...] = hlast

    y_scr[...] = (b_scr[...] * _gelu_tanh(gb_scr[...])).astype(bf16)
    out = jnp.dot(y_scr[...], wout_ref[...], preferred_element_type=f32)
    o_ref[...] = x + _gate(out, gt_ref[0], group)


def _lru_layer(x, mod, mod_rows, layer, group, steps, gn, win, cw, cb, wg, bgx, bga, lp, wout, h0, c0,
               batch_major_in):
    d = x.shape[-1]
    rows_total = x.shape[0] * x.shape[1] if batch_major_in else x.shape[0]
    d_lru = wout.shape[1]
    n_conv = cw.shape[1]
    rows = group * steps
    halo = (n_conv - 1) * group
    spec = lambda p: _mod_spec(group, mod_rows, layer, p, d)
    par = lambda a: _layer_spec(a.shape, layer)
    tile = pl.BlockSpec((rows, d), lambda t: (t, 0))
    if batch_major_in:
        x_spec = pl.BlockSpec((group, steps, d), lambda t: (0, t, 0))
        slabs = [pltpu.VMEM((d // LANES, rows, LANES), f32)]
    else:
        x_spec, slabs = tile, []
    kernel = functools.partial(_lru_kernel, group, steps, n_conv, batch_major_in)
    return pl.pallas_call(
        kernel,
        out_shape=(jax.ShapeDtypeStruct((rows_total, d), f32),
                   jax.ShapeDtypeStruct((group, d_lru), f32),
                   jax.ShapeDtypeStruct(c0.shape, f32)),
        grid=(rows_total // rows,),
        in_specs=[x_spec, spec(0), spec(1), spec(2), par(gn), par(win), par(cw), par(cb), par(wg), par(bgx),
                  par(bga), par(lp), par(wout), _const_spec(h0.shape), _const_spec(c0.shape)],
        out_specs=(tile, _const_spec((group, d_lru)), _const_spec(c0.shape)),
        scratch_shapes=[pltpu.VMEM((rows + halo, d_lru), f32), pltpu.VMEM((halo, d_lru), f32),
                        pltpu.VMEM((rows, d_lru), f32), pltpu.VMEM((rows, d_lru), f32),
                        pltpu.VMEM((rows, d_lru), f32), pltpu.VMEM((rows, d_lru), bf16),
                        pltpu.VMEM((group, d_lru), f32)] + slabs,
        compiler_params=_params("arbitrary"),
        name="rglru_layer",
    )(x, mod, mod, mod, gn, win, cw, cb, wg, bgx, bga, lp, wout, h0, c0)


def _ffn_kernel(group, steps, n_conv, fc, batch_major_out,
                x_ref, sh_ref, sc_ref, gt_ref, gn_ref, wup_ref, cw_ref, cb_ref, wd_ref, c0_ref,
                o_ref, cn_ref,
                uwa_scr, uwg_scr, halo_scr, *slabs):
    rows = group * steps
    halo = (n_conv - 1) * group
    d_ff = wd_ref.shape[0]

    @pl.when(pl.program_id(0) == 0)
    def _():
        halo_scr[...] = c0_ref[...]

    x = x_ref[...]
    h = _mod_norm(x, gn_ref[...], sh_ref[0], sc_ref[0], group).astype(bf16)

    def conv_half(col0, uw_scr):
        cols = slice(col0, col0 + fc)
        uw_scr[0:halo, :] = halo_scr[:, cols]
        uw_scr[halo:halo + rows, :] = jnp.dot(h, wup_ref[:, cols], preferred_element_type=f32)
        y = cb_ref[:, cols] + cw_ref[0:1, cols] * uw_scr[0:rows, :]
        for j in range(1, n_conv):
            y = y + cw_ref[j:j + 1, cols] * uw_scr[j * group:j * group + rows, :]
        new_halo = uw_scr[rows:rows + halo, :]
        halo_scr[:, cols] = new_halo
        cn_ref[:, cols] = new_halo
        return y

    acc = jnp.zeros(x.shape, f32)
    for c in range(d_ff // fc):
        a = conv_half(c * fc, uwa_scr)
        g = conv_half(d_ff + c * fc, uwg_scr)
        act = (_gelu_tanh(a) * g).astype(bf16)
        acc = acc + jnp.dot(act, wd_ref[c * fc:(c + 1) * fc, :], preferred_element_type=f32)

    y = x + _gate(acc, gt_ref[0], group)
    if batch_major_out:
        for j in range(slabs[0].shape[0]):
            slabs[0][j] = y[:, j * LANES:(j + 1) * LANES]
        _store_batch_major(o_ref, slabs[0], group)
    else:
        o_ref[...] = y


def _ffn_layer(x, mod, mod_rows, layer, group, steps, fc, gn, wup, cw, cb, wdown, c0, batch_major_out):
    rows_total, d = x.shape
    d_ff = wdown.shape[1]
    n_conv = cw.shape[1]
    rows = group * steps
    halo = (n_conv - 1) * group
    spec = lambda p: _mod_spec(group, mod_rows, layer, p, d)
    tile = pl.BlockSpec((rows, d), lambda t: (t, 0))
    resident = lambda a: pl.BlockSpec((None,) + tuple(a.shape[1:]), lambda t: (layer,) + (0,) * (a.ndim - 1),
                                      pipeline_mode=pl.Buffered(1))
    state = pl.BlockSpec(c0.shape, lambda t: (0, 0))
    if batch_major_out:
        out_shape = jax.ShapeDtypeStruct((group, rows_total // group, d), f32)
        out_spec = pl.BlockSpec((group, steps, d), lambda t: (0, t, 0))
        slabs = [pltpu.VMEM((d // LANES, rows, LANES), f32)]
    else:
        out_shape, out_spec, slabs = jax.ShapeDtypeStruct((rows_total, d), f32), tile, []
    kernel = functools.partial(_ffn_kernel, group, steps, n_conv, fc, batch_major_out)
    return pl.pallas_call(
        kernel,
        out_shape=(out_shape, jax.ShapeDtypeStruct(c0.shape, f32)),
        grid=(rows_total // rows,),
        in_specs=[tile, spec(3), spec(4), spec(5), _layer_spec(gn.shape, layer),
                  resident(wup), resident(cw), resident(cb), resident(wdown), state],
        out_specs=(out_spec, state),
        scratch_shapes=[pltpu.VMEM((rows + halo, fc), f32), pltpu.VMEM((rows + halo, fc), f32),
                        pltpu.VMEM((halo, 2 * d_ff), f32)] + slabs,
        compiler_params=_params("arbitrary"),
        name="conv_ffn",
    )(x, mod, mod, mod, gn, wup, cw, cb, wdown, c0)


def _q_kernel(group, scale, batch_major, x_ref, sh_ref, sc_ref, gn_ref, w_ref, gq_ref, o_ref, *slabs):
    h = _mod_norm(x_ref[...], gn_ref[...], sh_ref[0], sc_ref[0], group).astype(bf16)
    q = jnp.dot(h, w_ref[...], preferred_element_type=f32)
    for j in range(q.shape[1] // LANES):
        cols = slice(j * LANES, (j + 1) * LANES)
        qj = _head_half_norm(q[:, cols], gq_ref[...]) * scale
        if batch_major:
            slabs[0][j] = qj
        else:
            o_ref[:, cols] = qj.astype(o_ref.dtype)
    if batch_major:
        _store_batch_major(o_ref, slabs[0], group)


def _q_proj(x, mod, mod_rows, layer, block, group, steps, gn, wq, gq2, scale, out_dtype, batch_major):
    rows_total, d = x.shape
    width = wq.shape[2]
    rows = group * steps
    spec = lambda p: _mod_spec(group, mod_rows, layer, p, d)
    if batch_major:
        out_shape = jax.ShapeDtypeStruct((group, rows_total // group, width), out_dtype)
        out_spec = pl.BlockSpec((group, steps, width), lambda t: (0, t, 0))
        scratch = [pltpu.VMEM((width // LANES, rows, LANES), f32)]
    else:
        out_shape = jax.ShapeDtypeStruct((rows_total, width), out_dtype)
        out_spec = pl.BlockSpec((rows, width), lambda t: (t, 0))
        scratch = []
    return pl.pallas_call(
        functools.partial(_q_kernel, group, scale, batch_major),
        out_shape=out_shape,
        grid=(rows_total // rows,),
        in_specs=[pl.BlockSpec((rows, d), lambda t: (t, 0)), spec(0), spec(1), _layer_spec(gn.shape, layer),
                  _layer_spec(wq.shape, block), _const_spec(gq2.shape)],
        out_specs=out_spec,
        scratch_shapes=scratch,
        compiler_params=_params("arbitrary"),
        name="q_proj",
    )(x, mod, mod, gn, wq, gq2)


def _kv_kernel(group, batch_major, x_ref, sh_ref, sc_ref, gn_ref, w_ref, gk_ref, k_ref, v_ref, *slabs):
    h = _mod_norm(x_ref[...], gn_ref[...], sh_ref[0], sc_ref[0], group).astype(bf16)
    kv = jnp.dot(h, w_ref[...], preferred_element_type=f32)
    width = kv.shape[1] // 2
    for j in range(width // LANES):
        cols = slice(j * LANES, (j + 1) * LANES)
        kj = _head_half_norm(kv[:, cols], gk_ref[...])
        vj = kv[:, width + j * LANES:width + (j + 1) * LANES]
        if batch_major:
            slabs[0][j] = kj
            slabs[1][j] = vj
        else:
            k_ref[:, cols] = kj
            v_ref[:, cols] = vj
    if batch_major:
        _store_batch_major(k_ref, slabs[0], group)
        _store_batch_major(v_ref, slabs[1], group)


def _kv_proj(x, mod_kv, mod_rows, group, steps, gn, wkv, gk2, batch_major):
    rows_total, d = x.shape
    width = wkv.shape[1] // 2
    rows = group * steps
    spec = lambda p: _mod_spec(group, mod_rows, 0, p, d)
    if batch_major:
        shape = (group, rows_total // group, width)
        out_tile = pl.BlockSpec((group, steps, width), lambda t: (0, t, 0))
        scratch = [pltpu.VMEM((width // LANES, rows, LANES), f32)] * 2
    else:
        shape = (rows_total, width)
        out_tile = pl.BlockSpec((rows, width), lambda t: (t, 0))
        scratch = []
    return pl.pallas_call(
        functools.partial(_kv_kernel, group, batch_major),
        out_shape=(jax.ShapeDtypeStruct(shape, f32), jax.ShapeDtypeStruct(shape, f32)),
        grid=(rows_total // rows,),
        in_specs=[pl.BlockSpec((rows, d), lambda t: (t, 0)), spec(0), spec(1), _const_spec(gn.shape),
                  _const_spec(wkv.shape), _const_spec(gk2.shape)],
        out_specs=(out_tile, out_tile),
        scratch_shapes=scratch,
        compiler_params=_params("arbitrary"),
        name="kv_proj",
    )(x, mod_kv, mod_kv, gn, wkv, gk2)


def _o_kernel(group, batch_major, a_ref, x_ref, gt_ref, w_ref, o_ref, *slabs):
    if batch_major:
        a = _load_step_major(a_ref, slabs[0], group).astype(bf16)
    else:
        a = a_ref[...]
    out = jnp.dot(a, w_ref[...], preferred_element_type=f32)
    o_ref[...] = x_ref[...] + _gate(out, gt_ref[0], group)


def _o_proj(attn, x, mod, mod_rows, layer, block, group, steps, wo, batch_major):
    rows_total, d = x.shape
    width = wo.shape[1]
    rows = group * steps
    tile = pl.BlockSpec((rows, d), lambda t: (t, 0))
    if batch_major:
        a_spec = pl.BlockSpec((group, steps, width), lambda t: (0, t, 0))
        scratch = [pltpu.VMEM((width // LANES, rows, LANES), f32)]
    else:
        a_spec = pl.BlockSpec((rows, width), lambda t: (t, 0))
        scratch = []
    return pl.pallas_call(
        functools.partial(_o_kernel, group, batch_major),
        out_shape=jax.ShapeDtypeStruct((rows_total, d), f32),
        grid=(rows_total // rows,),
        in_specs=[a_spec, tile, _mod_spec(group, mod_rows, layer, 2, d), _layer_spec(wo.shape, block)],
        out_specs=tile,
        scratch_shapes=scratch,
        compiler_params=_params("arbitrary"),
        name="o_proj",
    )(attn, x, mod, wo)


def _lambda(lp, lam_init):
    s1 = jnp.sum(lp[0:1] * lp[1:2], axis=-1, keepdims=True)
    s2 = jnp.sum(lp[2:3] * lp[3:4], axis=-1, keepdims=True)
    return jnp.exp(s1) - jnp.exp(s2) + lam_init


def _sub_norm(o, gain, lam_init):
    return o * lax.rsqrt(jnp.mean(o * o, axis=-1, keepdims=True) + EPS) * gain * (1.0 - lam_init)


def _prompt_attn_kernel(lam_init, tile, slopes_ref, lp_ref, gs_ref, q_ref, k_ref, v_ref, o_ref,
                        bias_scr, k16_scr, v16_scr):
    head = pl.program_id(0)
    seq = pl.program_id(1)
    slope2 = slopes_ref[head] * LOG2E
    n_tiles = q_ref.shape[0] // tile

    @pl.when(seq == 0)
    def _():
        dist = (lax.broadcasted_iota(jnp.int32, (tile, tile), 0)
                - lax.broadcasted_iota(jnp.int32, (tile, tile), 1))
        for n in range(n_tiles):
            bias = -slope2 * (dist + n * tile).astype(f32)
            if n == 0:
                bias = jnp.where(dist >= 0, bias, MASKED)
            for c in range(2):
                bias_scr[n, c * tile:(c + 1) * tile, :] = bias

    k16_scr[...] = k_ref[...].astype(bf16)
    v16_scr[...] = v_ref[...].astype(bf16)
    lam = _lambda(lp_ref[...], lam_init)
    for qi in range(n_tiles):
        q = q_ref[qi * tile:(qi + 1) * tile, :]
        lane = lax.broadcasted_iota(jnp.int32, q.shape, 1)
        zero = jnp.zeros_like(q)
        q_maps = jnp.concatenate([jnp.where(lane < LANES // 2, q, zero),
                                  jnp.where(lane < LANES // 2, zero, q)], axis=0)
        for j in range(qi + 1):
            kt = k16_scr[j * tile:(j + 1) * tile, :]
            vt = v16_scr[j * tile:(j + 1) * tile, :]
            s = lax.dot_general(q_maps, kt, (((1,), (1,)), ((), ())), preferred_element_type=f32)
            s = s + bias_scr[qi - j]
            row_max = jnp.max(s, axis=-1, keepdims=True)
            m_new = row_max if j == 0 else jnp.maximum(m, row_max)
            p = jnp.exp2(s - m_new)
            p_lanes = p[:, 0:LANES]
            for i in range(1, tile // LANES):
                p_lanes = p_lanes + p[:, i * LANES:(i + 1) * LANES]
            pv = jnp.dot(p.astype(bf16), vt, preferred_element_type=f32)
            if j == 0:
                l, acc = p_lanes, pv
            else:
                alpha = jnp.exp2(m - m_new)
                l = alpha * l + p_lanes
                acc = alpha * acc + pv
            m = m_new
        normed = acc / jnp.sum(l, axis=-1, keepdims=True)
        o = normed[:tile] - lam * normed[tile:]
        o_ref[qi * tile:(qi + 1) * tile, :] = _sub_norm(o, gs_ref[...], lam_init).astype(o_ref.dtype)


def _prompt_attention(q, k, v, slopes, lam_params, gsub, lam_init, n_heads, tile):
    n_seq, t_len, _ = q.shape
    spec = pl.BlockSpec((None, t_len, LANES), lambda h, g: (g, 0, h))
    return pl.pallas_call(
        functools.partial(_prompt_attn_kernel, lam_init, tile),
        out_shape=jax.ShapeDtypeStruct(q.shape, bf16),
        grid=(n_heads, n_seq),
        in_specs=[pl.BlockSpec(memory_space=pltpu.SMEM), _const_spec(lam_params.shape), _const_spec(gsub.shape),
                  spec, spec, spec],
        out_specs=spec,
        scratch_shapes=[pltpu.VMEM((t_len // tile, 2 * tile, tile), f32),
                        pltpu.VMEM((t_len, LANES), bf16), pltpu.VMEM((t_len, LANES), bf16)],
        compiler_params=_params("arbitrary", "arbitrary"),
        name="prompt_attention",
    )(slopes, lam_params, gsub, q, k, v)


def _decode_attn_kernel(lam_init, n_pages, past_len,
                        pt_ref, lp_ref, gs_ref, slope_ref, qt_ref, q_ref, kn_ref, vn_ref, *refs):
    k_refs = refs[:n_pages]
    v_refs = refs[n_pages:2 * n_pages]
    o_ref = refs[2 * n_pages]
    qcol_scr, s_scr = refs[2 * n_pages + 1:]
    b = pl.program_id(0)
    _, n_heads, n_maps, head_dim, _ = k_refs[0].shape

    lane = lax.broadcasted_iota(jnp.int32, qt_ref.shape, 1)
    col = jnp.sum(jnp.where(lane == b, qt_ref[...], 0.0), axis=-1, keepdims=True)
    qcol_scr[...] = jnp.broadcast_to(col, qcol_scr.shape)

    for i in range(n_pages):
        for h in range(n_heads):
            for c in range(n_maps):
                r = (h * n_maps + c) * head_dim
                prod = k_refs[i][0, h, c] * qcol_scr[r:r + head_dim, :]
                s_scr[c * n_heads + h:c * n_heads + h + 1, i * PAGE_SIZE:(i + 1) * PAGE_SIZE] = (
                    jnp.sum(prod, axis=0, keepdims=True))

    s = s_scr[...]
    pos = lax.broadcasted_iota(jnp.int32, s.shape, 1)
    s = s - slope_ref[...] * (past_len - pos).astype(f32)
    prod = q_ref[0] * kn_ref[0]
    low = lax.broadcasted_iota(jnp.int32, prod.shape, 1) < head_dim
    s_self = jnp.concatenate([jnp.sum(jnp.where(low, prod, 0.0), axis=-1, keepdims=True),
                              jnp.sum(jnp.where(low, 0.0, prod), axis=-1, keepdims=True)], axis=0)
    m = jnp.maximum(jnp.max(s, axis=-1, keepdims=True), s_self)
    p = jnp.exp(s - m)
    p_self = jnp.exp(s_self - m)
    inv = 1.0 / (jnp.sum(p, axis=-1, keepdims=True) + p_self)
    lam = _lambda(lp_ref[...], lam_init)
    w = p[:n_heads] * inv[:n_heads] - lam * (p[n_heads:] * inv[n_heads:])
    w_self = p_self[:n_heads] * inv[:n_heads] - lam * (p_self[n_heads:] * inv[n_heads:])

    lane = lax.broadcasted_iota(jnp.int32, (n_heads, PAGE_SIZE), 1)
    ones = jnp.ones((PAGE_SIZE, LANES), bf16)
    acc = w_self * vn_ref[0]
    for i in range(n_pages):
        w_page = w[:, i * PAGE_SIZE:(i + 1) * PAGE_SIZE]
        one_token = jnp.concatenate([jnp.where(lane == t, w_page, 0.0) for t in range(PAGE_SIZE)], axis=0)
        spread = jnp.dot(one_token.astype(bf16), ones, preferred_element_type=f32)
        acc = acc + jnp.sum(spread.reshape(PAGE_SIZE, n_heads, LANES) * v_refs[i][...], axis=0)
    o_ref[0] = _sub_norm(acc, gs_ref[...], lam_init).astype(o_ref.dtype)


def _decode_attention(q, k_new, v_new, cache_k, cache_v, page_table, slopes_np, lam_params, gsub, lam_init):
    n_b, width = q.shape
    n_pool, page, n_heads, n_maps, head_dim = cache_k.shape
    n_pg = page_table.shape[1]
    past_len = n_pg * page
    head_w = n_maps * head_dim
    k_pages = jnp.transpose(cache_k, (0, 2, 3, 4, 1))
    v_tokens = cache_v.reshape(n_pool * page, n_heads, head_w)
    heads = lambda a: a.reshape(n_b, n_heads, head_w)
    slope_rows = jnp.asarray(np.tile(slopes_np, n_maps)[:, None], f32)
    pt_flat = page_table.reshape(-1)
    consts = [lam_params, gsub, slope_rows, q.T]
    sample = pl.BlockSpec((1, n_heads, head_w), lambda b, pt: (b, 0, 0))
    key_page = lambda i: pl.BlockSpec((1, n_heads, n_maps, head_dim, page),
                                      lambda b, pt: (pt[b * n_pg + i], 0, 0, 0, 0))
    value_page = lambda i: pl.BlockSpec((page, n_heads, head_w), lambda b, pt: (pt[b * n_pg + i], 0, 0))
    grid_spec = pltpu.PrefetchScalarGridSpec(
        num_scalar_prefetch=1,
        grid=(n_b,),
        in_specs=[pl.BlockSpec(a.shape, lambda b, pt, n=a.ndim: (0,) * n) for a in consts]
                 + [sample, sample, sample]
                 + [key_page(i) for i in range(n_pg)]
                 + [value_page(i) for i in range(n_pg)],
        out_specs=sample,
        scratch_shapes=[pltpu.VMEM((width, LANES), f32),
                        pltpu.VMEM((n_maps * n_heads, past_len), f32)],
    )
    return pl.pallas_call(
        functools.partial(_decode_attn_kernel, lam_init, n_pg, past_len),
        out_shape=jax.ShapeDtypeStruct((n_b, n_heads, head_w), bf16),
        grid_spec=grid_spec,
        compiler_params=_params("arbitrary"),
        name="decode_attention",
    )(pt_flat, *consts, heads(q), heads(k_new), heads(v_new), *([k_pages] * n_pg), *([v_tokens] * n_pg))


def _time_major(a):
    return jnp.swapaxes(a, 0, 1).reshape((a.shape[0] * a.shape[1],) + a.shape[2:])


def _batch_major(a, n_seq):
    return jnp.swapaxes(a.reshape((a.shape[0] // n_seq, n_seq) + a.shape[1:]), 0, 1)


def kernel(x_prompt, x_sample, c_prompt, c_sample, cache_k, cache_v, page_table, state_lru_h, state_lru_conv, state_ffn_conv, w_ada, b_ada, g_norm_mix, g_norm_ffn, w_lru_in, w_lru_conv, b_lru_conv, w_gate_x, b_gate_x, w_gate_a, b_gate_a, lru_log_param, w_lru_out, w_ada_kv, b_ada_kv, g_norm_kv, w_kv, g_k_norm, w_q, g_q_norm, lam_q1, lam_k1, lam_q2, lam_k2, g_subln, w_o, w_up, w_ffn_conv, b_ffn_conv, w_down):
    depth, d, _ = w_ada.shape
    n_a = w_lru_in.shape[0]
    n_bp, seq, _ = x_prompt.shape
    n_bs, dec_seq, _ = x_sample.shape
    assert dec_seq == 1 and n_bp == 8 and n_bs % 8 == 0 and n_bs <= LANES
    d_lru = w_lru_out.shape[1]
    d_ff = w_down.shape[1]
    head_dim = g_q_norm.shape[1]
    assert 2 * head_dim == LANES and cache_k.shape[1] == PAGE_SIZE
    attn_w = w_q.shape[2]
    n_heads = attn_w // LANES
    n_conv_lru = w_lru_conv.shape[1]
    n_conv_ffn = w_ffn_conv.shape[1]

    c_all = jnp.concatenate([c_sample, c_prompt, jnp.zeros((8, d), f32)], axis=0)
    mod = _ada(c_all, w_ada, b_ada.reshape(depth, 1, 6 * d), 1536)
    mod_kv = _ada(c_all, w_ada_kv[None], b_ada_kv.reshape(1, 1, 2 * d), 1024)
    prompt_rows = n_bs // n_bp

    win16 = w_lru_in.astype(bf16)
    wg16 = jnp.concatenate([w_gate_x, w_gate_a], axis=-1).astype(bf16)
    wout16 = w_lru_out.astype(bf16)
    wkv16 = w_kv.astype(bf16)
    wq16 = w_q.astype(bf16)
    wo16 = w_o.astype(bf16)
    wup16 = w_up.astype(bf16)
    wdown16 = w_down.astype(bf16)
    rows3 = lambda a: a[:, None, :]
    slopes_np = (2.0 ** (-8.0 * np.arange(1, n_heads + 1, dtype=np.float64) / n_heads)).astype(np.float32)
    slopes = jnp.asarray(slopes_np)
    head_scale = head_dim ** -0.5

    def run_group(x, group, mod_rows, steps_lru, steps_ffn, steps_proj, lru_h0, lru_conv0, ffn_conv0, attend,
                  batch_major):
        lru_h, lru_conv, ffn_conv = [], [], []
        kv = None
        for l in range(depth):
            if l < n_a:
                x, h_last, conv_new = _lru_layer(
                    x, mod, mod_rows, l, group, steps_lru, rows3(g_norm_mix), win16, w_lru_conv,
                    rows3(b_lru_conv), wg16, rows3(b_gate_x), rows3(b_gate_a),
                    rows3(lru_log_param), wout16, lru_h0[l], lru_conv0[l], batch_major and l == 0)
                lru_h.append(h_last)
                lru_conv.append(conv_new)
            else:
                j = l - n_a
                lam_init = 0.8 - 0.6 * math.exp(-0.3 * l)
                lam_params = jnp.stack([lam_q1[j], lam_k1[j], lam_q2[j], lam_k2[j]])
                gq2 = jnp.tile(g_q_norm[j], 2)[None]
                attn = attend(x, l, j, gq2, lam_params, g_subln[j][None], lam_init, kv)
                x = _o_proj(attn, x, mod, mod_rows, l, j, group, steps_proj, wo16, batch_major)
            x, fb = _ffn_layer(x, mod, mod_rows, l, group, steps_ffn, 1024, rows3(g_norm_ffn), wup16,
                               w_ffn_conv, rows3(b_ffn_conv), wdown16, ffn_conv0[l],
                               batch_major and l == depth - 1)
            ffn_conv.append(fb)
            if l == n_a - 1:
                kv = _kv_proj(x, mod_kv, mod_rows, group, steps_proj, g_norm_kv[None], wkv16,
                              jnp.tile(g_k_norm, 2)[None], batch_major)
        return x, kv[0], kv[1], lru_h, lru_conv, ffn_conv

    steps_proj_p = min(64, seq)

    def attend_prompt(x, l, j, gq2, lam_params, gsub, lam_init, kv):
        q = _q_proj(x, mod, prompt_rows, l, j, n_bp, steps_proj_p, rows3(g_norm_mix), wq16, gq2,
                    head_scale * LOG2E, bf16, True)
        return _prompt_attention(q, kv[0], kv[1], slopes, lam_params, gsub, lam_init, n_heads, min(256, seq))

    zeros_p = lambda n, r, w: [jnp.zeros((r, w), f32)] * n
    y_prompt, kp, vp, hp, lcp, fcp = run_group(
        x_prompt, n_bp, prompt_rows, min(64, seq), min(128, seq), steps_proj_p,
        zeros_p(n_a, n_bp, d_lru), zeros_p(n_a, (n_conv_lru - 1) * n_bp, d_lru),
        zeros_p(depth, (n_conv_ffn - 1) * n_bp, 2 * d_ff), attend_prompt, True)
    k_prompt = kp.reshape(n_bp, seq, n_heads, 2, head_dim)
    v_prompt = vp.reshape(n_bp, seq, n_heads, 2 * head_dim)
    state_bm = lambda s: jnp.stack([_batch_major(a, n_bp) for a in s])
    h_p = jnp.stack(hp)
    lc_p = state_bm(lcp)
    fc_p = state_bm(fcp)

    def attend_sample(x, l, j, gq2, lam_params, gsub, lam_init, kv):
        q = _q_proj(x, mod, 0, l, j, n_bs, 1, rows3(g_norm_mix), wq16, gq2, head_scale, f32, False)
        o = _decode_attention(q, kv[0], kv[1], cache_k, cache_v, page_table, slopes_np,
                              lam_params, gsub, lam_init)
        return o.reshape(n_bs, attn_w)

    xs = x_sample.reshape(n_bs, d)
    ys, ks, vs, hs, lcs, fcs = run_group(
        xs, n_bs, 0, 1, 1, 1,
        [state_lru_h[l] for l in range(n_a)],
        [_time_major(state_lru_conv[l]) for l in range(n_a)],
        [_time_major(state_ffn_conv[l]) for l in range(depth)], attend_sample, False)
    y_sample = ys.reshape(n_bs, 1, d)
    k_sample = ks.reshape(n_bs, 1, n_heads, 2, head_dim)
    v_sample = vs.reshape(n_bs, 1, n_heads, 2 * head_dim)
    state_bs = lambda s: jnp.stack([_batch_major(a, n_bs) for a in s])
    h_s = jnp.stack(hs)
    lc_s = state_bs(lcs)
    fc_s = state_bs(fcs)

    return (y_prompt, y_sample, k_prompt, v_prompt, k_sample, v_sample,
            h_p, h_s, lc_p, lc_s, fc_p, fc_s)
```

```python
import functools
import math

import numpy as np
import jax
import jax.numpy as jnp
from jax import lax
from jax.experimental import pallas as pl
from jax.experimental.pallas import tpu as pltpu

f32 = jnp.float32
bf16 = jnp.bfloat16

EPS = 1e-6
LRU_C = 8.0
LANES = 128
PAGE_SIZE = 128
MASKED = -1e30
LOG2E = math.log2(math.e)
VMEM_LIMIT_BYTES = 56 * 1024 * 1024


def _params(*semantics):
    return pltpu.CompilerParams(dimension_semantics=semantics,
                                vmem_limit_bytes=VMEM_LIMIT_BYTES)


def _const_spec(shape):
    zeros = (0,) * len(shape)
    return pl.BlockSpec(shape, lambda *_: zeros)


def _layer_spec(stacked_shape, layer):
    index = (layer,) + (0,) * (len(stacked_shape) - 1)
    return pl.BlockSpec((None,) + tuple(stacked_shape[1:]), lambda *_: index)


def _sigmoid(x):
    return 1.0 / (1.0 + jnp.exp(-x))


def _gelu_tanh(x):
    return 0.5 * x * (1.0 + jnp.tanh(math.sqrt(2.0 / math.pi) * (x + 0.044715 * (x * x * x))))


def _mod_norm(x, gain, shift, scale, group):
    rows, d = x.shape
    y = x * lax.rsqrt(jnp.mean(x * x, axis=-1, keepdims=True) + EPS) * gain
    y = y.reshape(rows // group, group, d) * (1.0 + scale)[None] + shift[None]
    return y.reshape(rows, d)


def _gate(x, gate, group):
    rows, d = x.shape
    return (x.reshape(rows // group, group, d) * gate[None]).reshape(rows, d)


def _head_half_norm(y, gain2):
    lane = lax.broadcasted_iota(jnp.int32, y.shape, 1)
    low = lane < (LANES // 2)
    sq = y * y
    s_low = jnp.sum(jnp.where(low, sq, 0.0), axis=-1, keepdims=True)
    s_high = jnp.sum(jnp.where(low, 0.0, sq), axis=-1, keepdims=True)
    inv = lax.rsqrt(jnp.where(low, s_low, s_high) * (2.0 / LANES) + EPS)
    return y * inv * gain2


def _store_batch_major(o_ref, slabs_ref, group):
    steps = slabs_ref.shape[1] // group
    for g in range(group):
        for j in range(slabs_ref.shape[0]):
            o_ref[g, :, j * LANES:(j + 1) * LANES] = (
                slabs_ref.at[j][pl.ds(g, steps, stride=group), :].astype(o_ref.dtype))


def _load_step_major(a_ref, slabs_ref, group):
    steps = a_ref.shape[1]
    n_slabs = slabs_ref.shape[0]
    for g in range(group):
        for j in range(n_slabs):
            slabs_ref.at[j][pl.ds(g, steps, stride=group), :] = (
                a_ref[g, :, j * LANES:(j + 1) * LANES].astype(f32))
    return jnp.concatenate([slabs_ref[j] for j in range(n_slabs)], axis=-1)


def _ada_kernel(c_ref, w_ref, b_ref, o_ref):
    c = c_ref[...]
    sc = (c * _sigmoid(c)).astype(bf16)
    o_ref[0] = jnp.dot(sc, w_ref[0].astype(bf16), preferred_element_type=f32) + b_ref[0]


def _ada(c, w, b, tn):
    n_layers, d, n = w.shape
    m = c.shape[0]
    return pl.pallas_call(
        _ada_kernel,
        out_shape=jax.ShapeDtypeStruct((n_layers, m, n), f32),
        grid=(n_layers, n // tn),
        in_specs=[_const_spec((m, d)),
                  pl.BlockSpec((1, d, tn), lambda l, j: (l, 0, j)),
                  pl.BlockSpec((1, 1, tn), lambda l, j: (l, 0, j))],
        out_specs=pl.BlockSpec((1, m, tn), lambda l, j: (l, 0, j)),
        compiler_params=_params("arbitrary", "arbitrary"),
        name="ada_mod",
    )(c, w, b)


def _mod_spec(group, row_block, layer, piece, d):
    return pl.BlockSpec((1, group, d), lambda *_: (layer, row_block, piece))


def _lru_kernel(group, steps, n_conv, batch_major_in,
                x_ref, sh_ref, sc_ref, gt_ref, gn_ref, win_ref, cw_ref, cb_ref,
                wg_ref, bgx_ref, bga_ref, lp_ref, wout_ref, h0_ref, c0_ref,
                o_ref, hlast_ref, cnew_ref,
                xw_scr, halo_scr, a_scr, b_scr, gb_scr, y_scr, h_scr, *slabs):
    rows = group * steps
    d_lru = a_scr.shape[1]
    halo = (n_conv - 1) * group
    t = pl.program_id(0)

    @pl.when(t == 0)
    def _():
        h_scr[...] = h0_ref[...]
        halo_scr[...] = c0_ref[...]

    x = _load_step_major(x_ref, slabs[0], group) if batch_major_in else x_ref[...]
    h = _mod_norm(x, gn_ref[...], sh_ref[0], sc_ref[0], group).astype(bf16)
    u = jnp.dot(h, win_ref[...], preferred_element_type=f32)
    gb_scr[...] = u[:, d_lru:]

    xw_scr[0:halo, :] = halo_scr[...]
    xw_scr[halo:halo + rows, :] = u[:, :d_lru]
    xc = cb_ref[...] + cw_ref[0:1, :] * xw_scr[0:rows, :]
    for j in range(1, n_conv):
        xc = xc + cw_ref[j:j + 1, :] * xw_scr[j * group:j * group + rows, :]
    new_halo = xw_scr[rows:rows + halo, :]
    halo_scr[...] = new_halo
    cnew_ref[...] = new_halo

    lp = lp_ref[...]
    log_sig = jnp.minimum(lp, 0.0) - jnp.log1p(jnp.exp(-jnp.abs(lp)))
    n_blocks = wg_ref.shape[0]
    blk = d_lru // n_blocks
    for n in range(n_blocks):
        cols = slice(n * blk, (n + 1) * blk)
        xn = xc[:, cols]
        gates = jnp.dot(xn.astype(bf16), wg_ref[n], preferred_element_type=f32)
        gate_x = _sigmoid(gates[:, :blk] + bgx_ref[:, cols])
        gate_a = _sigmoid(gates[:, blk:] + bga_ref[:, cols])
        log_a = LRU_C * gate_a * log_sig[:, cols]
        a = jnp.exp(log_a)
        a_scr[:, cols] = a
        b_scr[:, cols] = jnp.sqrt(-jnp.tanh(log_a) * (a * a + 1.0)) * (gate_x * xn)

    def step(i, hprev):
        r = pl.multiple_of(i * group, group)
        hnew = a_scr[pl.ds(r, group), :] * hprev + b_scr[pl.ds(r, group), :]
        b_scr[pl.ds(r, group), :] = hnew
        return hnew

    if steps == 1:
        hlast = step(0, h_scr[...])
    else:
        hlast = lax.fori_loop(0, steps, step, h_scr[...], unroll=8)
    h_scr[...] = hlast
    hlast_ref[...] = hlast

    y_scr[...] = (b_scr[...] * _gelu_tanh(gb_scr[...])).astype(bf16)
    out = jnp.dot(y_scr[...], wout_ref[...], preferred_element_type=f32)
    o_ref[...] = x + _gate(out, gt_ref[0], group)


def _lru_layer(x, mod, mod_rows, layer, group, steps, gn, win, cw, cb, wg, bgx, bga, lp, wout, h0, c0,
               batch_major_in):
    d = x.shape[-1]
    rows_total = x.shape[0] * x.shape[1] if batch_major_in else x.shape[0]
    d_lru = wout.shape[1]
    n_conv = cw.shape[1]
    rows = group * steps
    halo = (n_conv - 1) * group
    spec = lambda p: _mod_spec(group, mod_rows, layer, p, d)
    par = lambda a: _layer_spec(a.shape, layer)
    tile = pl.BlockSpec((rows, d), lambda t: (t, 0))
    if batch_major_in:
        x_spec = pl.BlockSpec((group, steps, d), lambda t: (0, t, 0))
        slabs = [pltpu.VMEM((d // LANES, rows, LANES), f32)]
    else:
        x_spec, slabs = tile, []
    kernel = functools.partial(_lru_kernel, group, steps, n_conv, batch_major_in)
    return pl.pallas_call(
        kernel,
        out_shape=(jax.ShapeDtypeStruct((rows_total, d), f32),
                   jax.ShapeDtypeStruct((group, d_lru), f32),
                   jax.ShapeDtypeStruct(c0.shape, f32)),
        grid=(rows_total // rows,),
        in_specs=[x_spec, spec(0), spec(1), spec(2), par(gn), par(win), par(cw), par(cb), par(wg), par(bgx),
                  par(bga), par(lp), par(wout), _const_spec(h0.shape), _const_spec(c0.shape)],
        out_specs=(tile, _const_spec((group, d_lru)), _const_spec(c0.shape)),
        scratch_shapes=[pltpu.VMEM((rows + halo, d_lru), f32), pltpu.VMEM((halo, d_lru), f32),
                        pltpu.VMEM((rows, d_lru), f32), pltpu.VMEM((rows, d_lru), f32),
                        pltpu.VMEM((rows, d_lru), f32), pltpu.VMEM((rows, d_lru), bf16),
                        pltpu.VMEM((group, d_lru), f32)] + slabs,
        compiler_params=_params("arbitrary"),
        name="rglru_layer",
    )(x, mod, mod, mod, gn, win, cw, cb, wg, bgx, bga, lp, wout, h0, c0)


def _ffn_kernel(group, steps, n_conv, fc, batch_major_out,
                x_ref, sh_ref, sc_ref, gt_ref, gn_ref, wup_ref, cw_ref, cb_ref, wd_ref, c0_ref,
                o_ref, cn_ref,
                uwa_scr, uwg_scr, halo_scr, *slabs):
    rows = group * steps
    halo = (n_conv - 1) * group
    d_ff = wd_ref.shape[0]

    @pl.when(pl.program_id(0) == 0)
    def _():
        halo_scr[...] = c0_ref[...]

    x = x_ref[...]
    h = _mod_norm(x, gn_ref[...], sh_ref[0], sc_ref[0], group).astype(bf16)

    def conv_half(col0, uw_scr):
        cols = slice(col0, col0 + fc)
        uw_scr[0:halo, :] = halo_scr[:, cols]
        uw_scr[halo:halo + rows, :] = jnp.dot(h, wup_ref[:, cols], preferred_element_type=f32)
        y = cb_ref[:, cols] + cw_ref[0:1, cols] * uw_scr[0:rows, :]
        for j in range(1, n_conv):
            y = y + cw_ref[j:j + 1, cols] * uw_scr[j * group:j * group + rows, :]
        new_halo = uw_scr[rows:rows + halo, :]
        halo_scr[:, cols] = new_halo
        cn_ref[:, cols] = new_halo
        return y

    acc = jnp.zeros(x.shape, f32)
    for c in range(d_ff // fc):
        a = conv_half(c * fc, uwa_scr)
        g = conv_half(d_ff + c * fc, uwg_scr)
        act = (_gelu_tanh(a) * g).astype(bf16)
        acc = acc + jnp.dot(act, wd_ref[c * fc:(c + 1) * fc, :], preferred_element_type=f32)

    y = x + _gate(acc, gt_ref[0], group)
    if batch_major_out:
        for j in range(slabs[0].shape[0]):
            slabs[0][j] = y[:, j * LANES:(j + 1) * LANES]
        _store_batch_major(o_ref, slabs[0], group)
    else:
        o_ref[...] = y


def _ffn_layer(x, mod, mod_rows, layer, group, steps, fc, gn, wup, cw, cb, wdown, c0, batch_major_out):
    rows_total, d = x.shape
    d_ff = wdown.shape[1]
    n_conv = cw.shape[1]
    rows = group * steps
    halo = (n_conv - 1) * group
    spec = lambda p: _mod_spec(group, mod_rows, layer, p, d)
    tile = pl.BlockSpec((rows, d), lambda t: (t, 0))
    resident = lambda a: pl.BlockSpec((None,) + tuple(a.shape[1:]), lambda t: (layer,) + (0,) * (a.ndim - 1),
                                      pipeline_mode=pl.Buffered(1))
    state = pl.BlockSpec(c0.shape, lambda t: (0, 0))
    if batch_major_out:
        out_shape = jax.ShapeDtypeStruct((group, rows_total // group, d), f32)
        out_spec = pl.BlockSpec((group, steps, d), lambda t: (0, t, 0))
        slabs = [pltpu.VMEM((d // LANES, rows, LANES), f32)]
    else:
        out_shape, out_spec, slabs = jax.ShapeDtypeStruct((rows_total, d), f32), tile, []
    kernel = functools.partial(_ffn_kernel, group, steps, n_conv, fc, batch_major_out)
    return pl.pallas_call(
        kernel,
        out_shape=(out_shape, jax.ShapeDtypeStruct(c0.shape, f32)),
        grid=(rows_total // rows,),
        in_specs=[tile, spec(3), spec(4), spec(5), _layer_spec(gn.shape, layer),
                  resident(wup), resident(cw), resident(cb), resident(wdown), state],
        out_specs=(out_spec, state),
        scratch_shapes=[pltpu.VMEM((rows + halo, fc), f32), pltpu.VMEM((rows + halo, fc), f32),
                        pltpu.VMEM((halo, 2 * d_ff), f32)] + slabs,
        compiler_params=_params("arbitrary"),
        name="conv_ffn",
    )(x, mod, mod, mod, gn, wup, cw, cb, wdown, c0)


def _q_kernel(group, scale, batch_major, x_ref, sh_ref, sc_ref, gn_ref, w_ref, gq_ref, o_ref, *slabs):
    h = _mod_norm(x_ref[...], gn_ref[...], sh_ref[0], sc_ref[0], group).astype(bf16)
    q = jnp.dot(h, w_ref[...], preferred_element_type=f32)
    for j in range(q.shape[1] // LANES):
        cols = slice(j * LANES, (j + 1) * LANES)
        qj = _head_half_norm(q[:, cols], gq_ref[...]) * scale
        if batch_major:
            slabs[0][j] = qj
        else:
            o_ref[:, cols] = qj.astype(o_ref.dtype)
    if batch_major:
        _store_batch_major(o_ref, slabs[0], group)


def _q_proj(x, mod, mod_rows, layer, block, group, steps, gn, wq, gq2, scale, out_dtype, batch_major):
    rows_total, d = x.shape
    width = wq.shape[2]
    rows = group * steps
    spec = lambda p: _mod_spec(group, mod_rows, layer, p, d)
    if batch_major:
        out_shape = jax.ShapeDtypeStruct((group, rows_total // group, width), out_dtype)
        out_spec = pl.BlockSpec((group, steps, width), lambda t: (0, t, 0))
        scratch = [pltpu.VMEM((width // LANES, rows, LANES), f32)]
    else:
        out_shape = jax.ShapeDtypeStruct((rows_total, width), out_dtype)
        out_spec = pl.BlockSpec((rows, width), lambda t: (t, 0))
        scratch = []
    return pl.pallas_call(
        functools.partial(_q_kernel, group, scale, batch_major),
        out_shape=out_shape,
        grid=(rows_total // rows,),
        in_specs=[pl.BlockSpec((rows, d), lambda t: (t, 0)), spec(0), spec(1), _layer_spec(gn.shape, layer),
                  _layer_spec(wq.shape, block), _const_spec(gq2.shape)],
        out_specs=out_spec,
        scratch_shapes=scratch,
        compiler_params=_params("arbitrary"),
        name="q_proj",
    )(x, mod, mod, gn, wq, gq2)


def _kv_kernel(group, batch_major, x_ref, sh_ref, sc_ref, gn_ref, w_ref, gk_ref, k_ref, v_ref, *slabs):
    h = _mod_norm(x_ref[...], gn_ref[...], sh_ref[0], sc_ref[0], group).astype(bf16)
    kv = jnp.dot(h, w_ref[...], preferred_element_type=f32)
    width = kv.shape[1] // 2
    for j in range(width // LANES):
        cols = slice(j * LANES, (j + 1) * LANES)
        kj = _head_half_norm(kv[:, cols], gk_ref[...])
        vj = kv[:, width + j * LANES:width + (j + 1) * LANES]
        if batch_major:
            slabs[0][j] = kj
            slabs[1][j] = vj
        else:
            k_ref[:, cols] = kj
            v_ref[:, cols] = vj
    if batch_major:
        _store_batch_major(k_ref, slabs[0], group)
        _store_batch_major(v_ref, slabs[1], group)


def _kv_proj(x, mod_kv, mod_rows, group, steps, gn, wkv, gk2, batch_major):
    rows_total, d = x.shape
    width = wkv.shape[1] // 2
    rows = group * steps
    spec = lambda p: _mod_spec(group, mod_rows, 0, p, d)
    if batch_major:
        shape = (group, rows_total // group, width)
        out_tile = pl.BlockSpec((group, steps, width), lambda t: (0, t, 0))
        scratch = [pltpu.VMEM((width // LANES, rows, LANES), f32)] * 2
    else:
        shape = (rows_total, width)
        out_tile = pl.BlockSpec((rows, width), lambda t: (t, 0))
        scratch = []
    return pl.pallas_call(
        functools.partial(_kv_kernel, group, batch_major),
        out_shape=(jax.ShapeDtypeStruct(shape, f32), jax.ShapeDtypeStruct(shape, f32)),
        grid=(rows_total // rows,),
        in_specs=[pl.BlockSpec((rows, d), lambda t: (t, 0)), spec(0), spec(1), _const_spec(gn.shape),
                  _const_spec(wkv.shape), _const_spec(gk2.shape)],
        out_specs=(out_tile, out_tile),
        scratch_shapes=scratch,
        compiler_params=_params("arbitrary"),
        name="kv_proj",
    )(x, mod_kv, mod_kv, gn, wkv, gk2)


def _o_kernel(group, batch_major, a_ref, x_ref, gt_ref, w_ref, o_ref, *slabs):
    if batch_major:
        a = _load_step_major(a_ref, slabs[0], group).astype(bf16)
    else:
        a = a_ref[...]
    out = jnp.dot(a, w_ref[...], preferred_element_type=f32)
    o_ref[...] = x_ref[...] + _gate(out, gt_ref[0], group)


def _o_proj(attn, x, mod, mod_rows, layer, block, group, steps, wo, batch_major):
    rows_total, d = x.shape
    width = wo.shape[1]
    rows = group * steps
    tile = pl.BlockSpec((rows, d), lambda t: (t, 0))
    if batch_major:
        a_spec = pl.BlockSpec((group, steps, width), lambda t: (0, t, 0))
        scratch = [pltpu.VMEM((width // LANES, rows, LANES), f32)]
    else:
        a_spec = pl.BlockSpec((rows, width), lambda t: (t, 0))
        scratch = []
    return pl.pallas_call(
        functools.partial(_o_kernel, group, batch_major),
        out_shape=jax.ShapeDtypeStruct((rows_total, d), f32),
        grid=(rows_total // rows,),
        in_specs=[a_spec, tile, _mod_spec(group, mod_rows, layer, 2, d), _layer_spec(wo.shape, block)],
        out_specs=tile,
        scratch_shapes=scratch,
        compiler_params=_params("arbitrary"),
        name="o_proj",
    )(attn, x, mod, wo)


def _lambda(lp, lam_init):
    s1 = jnp.sum(lp[0:1] * lp[1:2], axis=-1, keepdims=True)
    s2 = jnp.sum(lp[2:3] * lp[3:4], axis=-1, keepdims=True)
    return jnp.exp(s1) - jnp.exp(s2) + lam_init


def _sub_norm(o, gain, lam_init):
    return o * lax.rsqrt(jnp.mean(o * o, axis=-1, keepdims=True) + EPS) * gain * (1.0 - lam_init)


def _prompt_attn_kernel(lam_init, tile, slopes_ref, lp_ref, gs_ref, q_ref, k_ref, v_ref, o_ref,
                        bias_scr, k16_scr, v16_scr):
    head = pl.program_id(0)
    seq = pl.program_id(1)
    slope2 = slopes_ref[head] * LOG2E
    n_tiles = q_ref.shape[0] // tile

    @pl.when(seq == 0)
    def _():
        dist = (lax.broadcasted_iota(jnp.int32, (tile, tile), 0)
                - lax.broadcasted_iota(jnp.int32, (tile, tile), 1))
        for n in range(n_tiles):
            bias = -slope2 * (dist + n * tile).astype(f32)
            if n == 0:
                bias = jnp.where(dist >= 0, bias, MASKED)
            for c in range(2):
                bias_scr[n, c * tile:(c + 1) * tile, :] = bias

    k16_scr[...] = k_ref[...].astype(bf16)
    v16_scr[...] = v_ref[...].astype(bf16)
    lam = _lambda(lp_ref[...], lam_init)
    for qi in range(n_tiles):
        q = q_ref[qi * tile:(qi + 1) * tile, :]
        lane = lax.broadcasted_iota(jnp.int32, q.shape, 1)
        zero = jnp.zeros_like(q)
        q_maps = jnp.concatenate([jnp.where(lane < LANES // 2, q, zero),
                                  jnp.where(lane < LANES // 2, zero, q)], axis=0)
        for j in range(qi + 1):
            kt = k16_scr[j * tile:(j + 1) * tile, :]
            vt = v16_scr[j * tile:(j + 1) * tile, :]
            s = lax.dot_general(q_maps, kt, (((1,), (1,)), ((), ())), preferred_element_type=f32)
            s = s + bias_scr[qi - j]
            row_max = jnp.max(s, axis=-1, keepdims=True)
            m_new = row_max if j == 0 else jnp.maximum(m, row_max)
            p = jnp.exp2(s - m_new)
            p_lanes = p[:, 0:LANES]
            for i in range(1, tile // LANES):
                p_lanes = p_lanes + p[:, i * LANES:(i + 1) * LANES]
            pv = jnp.dot(p.astype(bf16), vt, preferred_element_type=f32)
            if j == 0:
                l, acc = p_lanes, pv
            else:
                alpha = jnp.exp2(m - m_new)
                l = alpha * l + p_lanes
                acc = alpha * acc + pv
            m = m_new
        normed = acc / jnp.sum(l, axis=-1, keepdims=True)
        o = normed[:tile] - lam * normed[tile:]
        o_ref[qi * tile:(qi + 1) * tile, :] = _sub_norm(o, gs_ref[...], lam_init).astype(o_ref.dtype)


def _prompt_attention(q, k, v, slopes, lam_params, gsub, lam_init, n_heads, tile):
    n_seq, t_len, _ = q.shape
    spec = pl.BlockSpec((None, t_len, LANES), lambda h, g: (g, 0, h))
    return pl.pallas_call(
        functools.partial(_prompt_attn_kernel, lam_init, tile),
        out_shape=jax.ShapeDtypeStruct(q.shape, bf16),
        grid=(n_heads, n_seq),
        in_specs=[pl.BlockSpec(memory_space=pltpu.SMEM), _const_spec(lam_params.shape), _const_spec(gsub.shape),
                  spec, spec, spec],
        out_specs=spec,
        scratch_shapes=[pltpu.VMEM((t_len // tile, 2 * tile, tile), f32),
                        pltpu.VMEM((t_len, LANES), bf16), pltpu.VMEM((t_len, LANES), bf16)],
        compiler_params=_params("arbitrary", "arbitrary"),
        name="prompt_attention",
    )(slopes, lam_params, gsub, q, k, v)


def _decode_attn_kernel(lam_init, n_pages, past_len,
                        pt_ref, lp_ref, gs_ref, slope_ref, qt_ref, q_ref, kn_ref, vn_ref, *refs):
    k_refs = refs[:n_pages]
    v_refs = refs[n_pages:2 * n_pages]
    o_ref = refs[2 * n_pages]
    qcol_scr, s_scr = refs[2 * n_pages + 1:]
    b = pl.program_id(0)
    _, n_heads, n_maps, head_dim, _ = k_refs[0].shape

    lane = lax.broadcasted_iota(jnp.int32, qt_ref.shape, 1)
    col = jnp.sum(jnp.where(lane == b, qt_ref[...], 0.0), axis=-1, keepdims=True)
    qcol_scr[...] = jnp.broadcast_to(col, qcol_scr.shape)

    for i in range(n_pages):
        for h in range(n_heads):
            for c in range(n_maps):
                r = (h * n_maps + c) * head_dim
                prod = k_refs[i][0, h, c] * qcol_scr[r:r + head_dim, :]
                s_scr[c * n_heads + h:c * n_heads + h + 1, i * PAGE_SIZE:(i + 1) * PAGE_SIZE] = (
                    jnp.sum(prod, axis=0, keepdims=True))

    s = s_scr[...]
    pos = lax.broadcasted_iota(jnp.int32, s.shape, 1)
    s = s - slope_ref[...] * (past_len - pos).astype(f32)
    prod = q_ref[0] * kn_ref[0]
    low = lax.broadcasted_iota(jnp.int32, prod.shape, 1) < head_dim
    s_self = jnp.concatenate([jnp.sum(jnp.where(low, prod, 0.0), axis=-1, keepdims=True),
                              jnp.sum(jnp.where(low, 0.0, prod), axis=-1, keepdims=True)], axis=0)
    m = jnp.maximum(jnp.max(s, axis=-1, keepdims=True), s_self)
    p = jnp.exp(s - m)
    p_self = jnp.exp(s_self - m)
    inv = 1.0 / (jnp.sum(p, axis=-1, keepdims=True) + p_self)
    lam = _lambda(lp_ref[...], lam_init)
    w = p[:n_heads] * inv[:n_heads] - lam * (p[n_heads:] * inv[n_heads:])
    w_self = p_self[:n_heads] * inv[:n_heads] - lam * (p_self[n_heads:] * inv[n_heads:])

    lane = lax.broadcasted_iota(jnp.int32, (n_heads, PAGE_SIZE), 1)
    ones = jnp.ones((PAGE_SIZE, LANES), bf16)
    acc = w_self * vn_ref[0]
    for i in range(n_pages):
        w_page = w[:, i * PAGE_SIZE:(i + 1) * PAGE_SIZE]
        one_token = jnp.concatenate([jnp.where(lane == t, w_page, 0.0) for t in range(PAGE_SIZE)], axis=0)
        spread = jnp.dot(one_token.astype(bf16), ones, preferred_element_type=f32)
        acc = acc + jnp.sum(spread.reshape(PAGE_SIZE, n_heads, LANES) * v_refs[i][...], axis=0)
    o_ref[0] = _sub_norm(acc, gs_ref[...], lam_init).astype(o_ref.dtype)


def _decode_attention(q, k_new, v_new, cache_k, cache_v, page_table, slopes_np, lam_params, gsub, lam_init):
    n_b, width = q.shape
    n_pool, page, n_heads, n_maps, head_dim = cache_k.shape
    n_pg = page_table.shape[1]
    past_len = n_pg * page
    head_w = n_maps * head_dim
    k_pages = jnp.transpose(cache_k, (0, 2, 3, 4, 1))
    v_tokens = cache_v.reshape(n_pool * page, n_heads, head_w)
    heads = lambda a: a.reshape(n_b, n_heads, head_w)
    slope_rows = jnp.asarray(np.tile(slopes_np, n_maps)[:, None], f32)
    pt_flat = page_table.reshape(-1)
    consts = [lam_params, gsub, slope_rows, q.T]
    sample = pl.BlockSpec((1, n_heads, head_w), lambda b, pt: (b, 0, 0))
    key_page = lambda i: pl.BlockSpec((1, n_heads, n_maps, head_dim, page),
                                      lambda b, pt: (pt[b * n_pg + i], 0, 0, 0, 0))
    value_page = lambda i: pl.BlockSpec((page, n_heads, head_w), lambda b, pt: (pt[b * n_pg + i], 0, 0))
    grid_spec = pltpu.PrefetchScalarGridSpec(
        num_scalar_prefetch=1,
        grid=(n_b,),
        in_specs=[pl.BlockSpec(a.shape, lambda b, pt, n=a.ndim: (0,) * n) for a in consts]
                 + [sample, sample, sample]
                 + [key_page(i) for i in range(n_pg)]
                 + [value_page(i) for i in range(n_pg)],
        out_specs=sample,
        scratch_shapes=[pltpu.VMEM((width, LANES), f32),
                        pltpu.VMEM((n_maps * n_heads, past_len), f32)],
    )
    return pl.pallas_call(
        functools.partial(_decode_attn_kernel, lam_init, n_pg, past_len),
        out_shape=jax.ShapeDtypeStruct((n_b, n_heads, head_w), bf16),
        grid_spec=grid_spec,
        compiler_params=_params("arbitrary"),
        name="decode_attention",
    )(pt_flat, *consts, heads(q), heads(k_new), heads(v_new), *([k_pages] * n_pg), *([v_tokens] * n_pg))


def _time_major(a):
    return jnp.swapaxes(a, 0, 1).reshape((a.shape[0] * a.shape[1],) + a.shape[2:])


def _batch_major(a, n_seq):
    return jnp.swapaxes(a.reshape((a.shape[0] // n_seq, n_seq) + a.shape[1:]), 0, 1)


def kernel(x_prompt, x_sample, c_prompt, c_sample, cache_k, cache_v, page_table, state_lru_h, state_lru_conv, state_ffn_conv, w_ada, b_ada, g_norm_mix, g_norm_ffn, w_lru_in, w_lru_conv, b_lru_conv, w_gate_x, b_gate_x, w_gate_a, b_gate_a, lru_log_param, w_lru_out, w_ada_kv, b_ada_kv, g_norm_kv, w_kv, g_k_norm, w_q, g_q_norm, lam_q1, lam_k1, lam_q2, lam_k2, g_subln, w_o, w_up, w_ffn_conv, b_ffn_conv, w_down):
    depth, d, _ = w_ada.shape
    n_a = w_lru_in.shape[0]
    n_bp, seq, _ = x_prompt.shape
    n_bs, dec_seq, _ = x_sample.shape
    assert dec_seq == 1 and n_bp == 8 and n_bs % 8 == 0 and n_bs <= LANES
    d_lru = w_lru_out.shape[1]
    d_ff = w_down.shape[1]
    head_dim = g_q_norm.shape[1]
    assert 2 * head_dim == LANES and cache_k.shape[1] == PAGE_SIZE
    attn_w = w_q.shape[2]
    n_heads = attn_w // LANES
    n_conv_lru = w_lru_conv.shape[1]
    n_conv_ffn = w_ffn_conv.shape[1]

    c_all = jnp.concatenate([c_sample, c_prompt, jnp.zeros((8, d), f32)], axis=0)
    mod = _ada(c_all, w_ada, b_ada.reshape(depth, 1, 6 * d), 1536)
    mod_kv = _ada(c_all, w_ada_kv[None], b_ada_kv.reshape(1, 1, 2 * d), 1024)
    prompt_rows = n_bs // n_bp

    win16 = w_lru_in.astype(bf16)
    wg16 = jnp.concatenate([w_gate_x, w_gate_a], axis=-1).astype(bf16)
    wout16 = w_lru_out.astype(bf16)
    wkv16 = w_kv.astype(bf16)
    wq16 = w_q.astype(bf16)
    wo16 = w_o.astype(bf16)
    wup16 = w_up.astype(bf16)
    wdown16 = w_down.astype(bf16)
    rows3 = lambda a: a[:, None, :]
    slopes_np = (2.0 ** (-8.0 * np.arange(1, n_heads + 1, dtype=np.float64) / n_heads)).astype(np.float32)
    slopes = jnp.asarray(slopes_np)
    head_scale = head_dim ** -0.5

    def run_group(x, group, mod_rows, steps_lru, steps_ffn, steps_proj, lru_h0, lru_conv0, ffn_conv0, attend,
                  batch_major):
        lru_h, lru_conv, ffn_conv = [], [], []
        kv = None
        for l in range(depth):
            if l < n_a:
                x, h_last, conv_new = _lru_layer(
                    x, mod, mod_rows, l, group, steps_lru, rows3(g_norm_mix), win16, w_lru_conv,
                    rows3(b_lru_conv), wg16, rows3(b_gate_x), rows3(b_gate_a),
                    rows3(lru_log_param), wout16, lru_h0[l], lru_conv0[l], batch_major and l == 0)
                lru_h.append(h_last)
                lru_conv.append(conv_new)
            else:
                j = l - n_a
                lam_init = 0.8 - 0.6 * math.exp(-0.3 * l)
                lam_params = jnp.stack([lam_q1[j], lam_k1[j], lam_q2[j], lam_k2[j]])
                gq2 = jnp.tile(g_q_norm[j], 2)[None]
                attn = attend(x, l, j, gq2, lam_params, g_subln[j][None], lam_init, kv)
                x = _o_proj(attn, x, mod, mod_rows, l, j, group, steps_proj, wo16, batch_major)
            x, fb = _ffn_layer(x, mod, mod_rows, l, group, steps_ffn, 1024, rows3(g_norm_ffn), wup16,
                               w_ffn_conv, rows3(b_ffn_conv), wdown16, ffn_conv0[l],
                               batch_major and l == depth - 1)
            ffn_conv.append(fb)
            if l == n_a - 1:
                kv = _kv_proj(x, mod_kv, mod_rows, group, steps_proj, g_norm_kv[None], wkv16,
                              jnp.tile(g_k_norm, 2)[None], batch_major)
        return x, kv[0], kv[1], lru_h, lru_conv, ffn_conv

    steps_proj_p = min(128, seq)

    def attend_prompt(x, l, j, gq2, lam_params, gsub, lam_init, kv):
        q = _q_proj(x, mod, prompt_rows, l, j, n_bp, steps_proj_p, rows3(g_norm_mix), wq16, gq2,
                    head_scale * LOG2E, bf16, True)
        return _prompt_attention(q, kv[0], kv[1], slopes, lam_params, gsub, lam_init, n_heads, min(256, seq))

    zeros_p = lambda n, r, w: [jnp.zeros((r, w), f32)] * n
    y_prompt, kp, vp, hp, lcp, fcp = run_group(
        x_prompt, n_bp, prompt_rows, min(64, seq), min(128, seq), steps_proj_p,
        zeros_p(n_a, n_bp, d_lru), zeros_p(n_a, (n_conv_lru - 1) * n_bp, d_lru),
        zeros_p(depth, (n_conv_ffn - 1) * n_bp, 2 * d_ff), attend_prompt, True)
    k_prompt = kp.reshape(n_bp, seq, n_heads, 2, head_dim)
    v_prompt = vp.reshape(n_bp, seq, n_heads, 2 * head_dim)
    state_bm = lambda s: jnp.stack([_batch_major(a, n_bp) for a in s])
    h_p = jnp.stack(hp)
    lc_p = state_bm(lcp)
    fc_p = state_bm(fcp)

    def attend_sample(x, l, j, gq2, lam_params, gsub, lam_init, kv):
        q = _q_proj(x, mod, 0, l, j, n_bs, 1, rows3(g_norm_mix), wq16, gq2, head_scale, f32, False)
        o = _decode_attention(q, kv[0], kv[1], cache_k, cache_v, page_table, slopes_np,
                              lam_params, gsub, lam_init)
        return o.reshape(n_bs, attn_w)

    xs = x_sample.reshape(n_bs, d)
    ys, ks, vs, hs, lcs, fcs = run_group(
        xs, n_bs, 0, 1, 1, 1,
        [state_lru_h[l] for l in range(n_a)],
        [_time_major(state_lru_conv[l]) for l in range(n_a)],
        [_time_major(state_ffn_conv[l]) for l in range(depth)], attend_sample, False)
    y_sample = ys.reshape(n_bs, 1, d)
    k_sample = ks.reshape(n_bs, 1, n_heads, 2, head_dim)
    v_sample = vs.reshape(n_bs, 1, n_heads, 2 * head_dim)
    state_bs = lambda s: jnp.stack([_batch_major(a, n_bs) for a in s])
    h_s = jnp.stack(hs)
    lc_s = state_bs(lcs)
    fc_s = state_bs(fcs)

    return (y_prompt, y_sample, k_prompt, v_prompt, k_sample, v_sample,
            h_p, h_s, lc_p, lc_s, fc_p, fc_s)
```
